```python
import jax, jax.numpy as jnp
from jax import lax
import numpy as np

D_MODEL = 1024
BATCH = 2
SEQ = 8192
DEPTH = 2

HEAD_DIM = 64
SB_HEADS = 8
SB_WIDTH = SB_HEADS * HEAD_DIM
SC_WIDTH = 512
SC_KERNEL = 3
DSA_HEADS = 8
DSA_WIDTH = DSA_HEADS * HEAD_DIM
IDX_HEADS = 8
IDX_DIM = 64
DSA_TOPK_MAX = 256
MEM_LEN = 256
XA_HEADS = 4
XA_HEAD_DIM = D_MODEL // XA_HEADS
D_FF = 2816
FFN_KERNEL = 3
Q_BLOCK = 128
N_BRANCH = 3
RMS_EPS = 1e-6
IN_SIZES = (3 * SB_WIDTH, 3 * SC_WIDTH, 3 * DSA_WIDTH, IDX_HEADS * IDX_DIM, IDX_DIM, IDX_HEADS, N_BRANCH * D_MODEL)
D_IN = sum(IN_SIZES)

kernel_name = "hybrid_stickbreak_shortconv_dsa_gated_block"

F32 = jnp.float32


def rmsnorm(x, g):
    xf = x.astype(F32)
    y = xf * lax.rsqrt(jnp.mean(xf * xf, axis=-1, keepdims=True) + RMS_EPS)
    return (y * g.astype(F32)).astype(x.dtype)


def causal_dwconv(u, w):
    K = w.shape[0]
    S = u.shape[1]
    up = jnp.pad(u, ((0, 0), (K - 1, 0), (0, 0)))
    return sum(w[k] * up[:, k:k + S] for k in range(K))


def to_blocks(a):
    B, S = a.shape[:2]
    return a.reshape((B, S // Q_BLOCK, Q_BLOCK) + a.shape[2:]).swapaxes(0, 1)


def from_blocks(a):
    a = a.swapaxes(0, 1)
    return a.reshape((a.shape[0], a.shape[1] * a.shape[2]) + a.shape[3:])


def stick_breaking_attention(q, k, v):
    B, S, H, Dh = q.shape
    nb = S // Q_BLOCK
    scale = Dh ** -0.5
    kf = k.astype(F32)
    vf = v.astype(F32)
    key_pos = jnp.arange(S)

    def block(args):
        qi, i = args
        z = jnp.einsum('bqhd,bshd->bhqs', qi.astype(F32), kf) * scale
        q_pos = i * Q_BLOCK + jnp.arange(Q_BLOCK)
        causal = key_pos[None, :] < q_pos[:, None]
        log_1m = jnp.where(causal, jax.nn.log_sigmoid(-z), 0.0)
        after = lax.cumsum(log_1m, axis=3, reverse=True) - log_1m
        w = jnp.where(causal, jnp.exp(jax.nn.log_sigmoid(z) + after), 0.0)
        return jnp.einsum('bhqs,bshd->bqhd', w, vf)

    out = lax.map(block, (to_blocks(q), jnp.arange(nb)))
    return from_blocks(out).astype(q.dtype)


def dsa_attention(q, k, v, q_idx, k_idx, w_idx):
    B, S, H, Dh = q.shape
    nb = S // Q_BLOCK
    top_k = min(DSA_TOPK_MAX, S // 4)
    kif = k_idx.astype(F32)
    key_pos = jnp.arange(S)

    def gather_rows(a, idx):
        return jax.vmap(lambda ab, ib: ab[ib])(a, idx)

    def block(args):
        qi, qii, wi, i = args
        q_pos = i * Q_BLOCK + jnp.arange(Q_BLOCK)
        idx_logits = jnp.einsum('bqhd,bsd->bqhs', qii.astype(F32), kif) * (IDX_DIM ** -0.5)
        score = jnp.einsum('bqh,bqhs->bqs', wi.astype(F32) * (IDX_HEADS ** -0.5), jax.nn.relu(idx_logits))
        admissible = key_pos[None, :] <= q_pos[:, None]
        score = jnp.where(admissible[None], score, -jnp.inf)
        _, sel = lax.top_k(score, top_k)
        valid = sel <= q_pos[None, :, None]
        flat = sel.reshape(B, Q_BLOCK * top_k)
        ks = gather_rows(k, flat).reshape(B, Q_BLOCK, top_k, H, Dh).astype(F32)
        vs = gather_rows(v, flat).reshape(B, Q_BLOCK, top_k, H, Dh).astype(F32)
        logits = jnp.einsum('bqhd,bqkhd->bhqk', qi.astype(F32), ks) * (Dh ** -0.5)
        logits = jnp.where(valid[:, None], logits, -jnp.inf)
        p = jax.nn.softmax(logits, axis=-1)
        return jnp.einsum('bhqk,bqkhd->bqhd', p, vs)

    out = lax.map(block, (to_blocks(q), to_blocks(q_idx), to_blocks(w_idx), jnp.arange(nb)))
    return from_blocks(out).astype(q.dtype)


def hybrid_mixer(h, w_in, sc_conv_w, w_sb_out, w_sc_out, w_dsa_out, w_mix_o):
    B, S, _ = h.shape
    z = h @ w_in
    sb_qkv, sc_bcx, dsa_qkv, idx_q, idx_k, idx_w, gates = jnp.split(z, list(np.cumsum(IN_SIZES)[:-1]), axis=-1)
    sb = sb_qkv.reshape(B, S, 3, SB_HEADS, HEAD_DIM)
    y_a = stick_breaking_attention(sb[:, :, 0], sb[:, :, 1], sb[:, :, 2]).reshape(B, S, SB_WIDTH)
    b_gate, c_gate, x_in = jnp.split(sc_bcx, 3, axis=-1)
    y_b = b_gate * causal_dwconv(c_gate * x_in, sc_conv_w)
    dq = dsa_qkv.reshape(B, S, 3, DSA_HEADS, HEAD_DIM)
    y_c = dsa_attention(dq[:, :, 0], dq[:, :, 1], dq[:, :, 2],
                        idx_q.reshape(B, S, IDX_HEADS, IDX_DIM), idx_k, idx_w).reshape(B, S, DSA_WIDTH)
    g = jax.nn.sigmoid(gates.reshape(B, S, N_BRANCH, D_MODEL).astype(F32)).astype(h.dtype)
    merged = g[:, :, 0] * (y_a @ w_sb_out) + g[:, :, 1] * (y_b @ w_sc_out) + g[:, :, 2] * (y_c @ w_dsa_out)
    return merged @ w_mix_o


def memory_cross_attention(h, mem_n, w_q, w_kv, w_o):
    B, S, _ = h.shape
    M = mem_n.shape[1]
    q = (h @ w_q).reshape(B, S, XA_HEADS, XA_HEAD_DIM).astype(F32)
    kv = (mem_n @ w_kv).reshape(B, M, 2, XA_HEADS, XA_HEAD_DIM).astype(F32)
    logits = jnp.einsum('bshd,bmhd->bhsm', q, kv[:, :, 0]) * (XA_HEAD_DIM ** -0.5)
    p = jax.nn.softmax(logits, axis=-1)
    o = jnp.einsum('bhsm,bmhd->bshd', p, kv[:, :, 1]).astype(h.dtype)
    return o.reshape(B, S, D_MODEL) @ w_o


def conv_ffn(h, w_in, conv_w, conv_b, w_down):
    a, u = jnp.split(h @ w_in, 2, axis=-1)
    a = causal_dwconv(a, conv_w) + conv_b
    return (jax.nn.gelu(a) * u) @ w_down


def setup_inputs(seed: int = 0) -> dict:
    key = jax.random.key(seed)
    ks = jax.random.split(key, 24)

    def nrm(k, shape, fan_in):
        return jax.random.normal(k, shape, F32) * (fan_in ** -0.5)

    def gain(k, shape):
        return 1.0 + 0.05 * jax.random.normal(k, shape, F32)

    L = DEPTH
    return {
        "x": jax.random.normal(ks[0], (BATCH, SEQ, D_MODEL), F32),
        "mem": jax.random.normal(ks[1], (BATCH, MEM_LEN, D_MODEL), F32),
        "norm_mix": gain(ks[2], (L, D_MODEL)),
        "w_in": nrm(ks[3], (L, D_MODEL, D_IN), D_MODEL),
        "sc_conv_w": nrm(ks[4], (L, SC_KERNEL, SC_WIDTH), SC_KERNEL),
        "w_sb_out": nrm(ks[5], (L, SB_WIDTH, D_MODEL), SB_WIDTH),
        "w_sc_out": nrm(ks[6], (L, SC_WIDTH, D_MODEL), SC_WIDTH),
        "w_dsa_out": nrm(ks[7], (L, DSA_WIDTH, D_MODEL), DSA_WIDTH),
        "w_mix_o": nrm(ks[8], (L, D_MODEL, D_MODEL), D_MODEL),
        "norm_xa": gain(ks[9], (L, D_MODEL)),
        "norm_mem": gain(ks[10], (L, D_MODEL)),
        "w_xa_q": nrm(ks[11], (L, D_MODEL, D_MODEL), D_MODEL),
        "w_xa_kv": nrm(ks[12], (L, D_MODEL, 2 * D_MODEL), D_MODEL),
        "w_xa_o": nrm(ks[13], (L, D_MODEL, D_MODEL), D_MODEL),
        "norm_ffn": gain(ks[14], (L, D_MODEL)),
        "w_ffn_in": nrm(ks[15], (L, D_MODEL, 2 * D_FF), D_MODEL),
        "ffn_conv_w": nrm(ks[16], (L, FFN_KERNEL, D_FF), FFN_KERNEL),
        "ffn_conv_b": 0.02 * jax.random.normal(ks[17], (L, D_FF), F32),
        "w_ffn_down": nrm(ks[18], (L, D_FF, D_MODEL), D_FF),
        "norm_final": gain(ks[19], (D_MODEL,)),
    }


def reference(x, mem, norm_mix, w_in, sc_conv_w, w_sb_out, w_sc_out, w_dsa_out, w_mix_o,
              norm_xa, norm_mem, w_xa_q, w_xa_kv, w_xa_o,
              norm_ffn, w_ffn_in, ffn_conv_w, ffn_conv_b, w_ffn_down, norm_final):
    for l in range(DEPTH):
        h = rmsnorm(x, norm_mix[l])
        x = x + hybrid_mixer(h, w_in[l], sc_conv_w[l], w_sb_out[l], w_sc_out[l], w_dsa_out[l], w_mix_o[l])
        h = rmsnorm(x, norm_xa[l])
        x = x + memory_cross_attention(h, rmsnorm(mem, norm_mem[l]), w_xa_q[l], w_xa_kv[l], w_xa_o[l])
        h = rmsnorm(x, norm_ffn[l])
        x = x + conv_ffn(h, w_ffn_in[l], ffn_conv_w[l], ffn_conv_b[l], w_ffn_down[l])
    return rmsnorm(x, norm_final)
```

```python
import functools

import jax
import jax.numpy as jnp
from jax import lax
from jax.experimental import pallas as pl
from jax.experimental.pallas import tpu as pltpu

F32 = jnp.float32
BF16 = jnp.bfloat16
I32 = jnp.int32

D_MODEL = 1024
HEAD_DIM = 64
N_HEADS = 8
WIDTH = N_HEADS * HEAD_DIM
IDX_DIM = 64
TOPK_MAX = 256
MEM_LEN = 256
XA_HEADS = 4
XA_HEAD_DIM = D_MODEL // XA_HEADS
D_FF = 2816
RMS_EPS = 1e-6
HALO = 8
INT_MIN = -2 ** 31
NEG_BIG = -1e30
SB_DEAD = -104.0
VMEM_LIMIT = 56 * 1024 * 1024

NT_DIMS = (((1,), (1,)), ((), ()))


def _cparams(*sem):
    return pltpu.CompilerParams(dimension_semantics=sem, vmem_limit_bytes=VMEM_LIMIT)


def _rms(x, g):
    ms = jnp.mean(x * x, axis=-1, keepdims=True)
    return x * lax.rsqrt(ms + RMS_EPS) * g


def _norm_matmul_kernel(x_ref, g_ref, w_ref, o_ref, h_ref):
    @pl.when(pl.program_id(1) == 0)
    def _():
        h_ref[...] = _rms(x_ref[...], g_ref[...]).astype(BF16)

    o_ref[...] = jnp.dot(h_ref[...], w_ref[...], preferred_element_type=F32).astype(o_ref.dtype)


def _norm_matmul(x, g, w, out_dtype, tm, tn):
    m, k = x.shape
    n = w.shape[1]
    return pl.pallas_call(
        _norm_matmul_kernel,
        grid=(m // tm, n // tn),
        in_specs=[
            pl.BlockSpec((tm, k), lambda i, j: (i, 0)),
            pl.BlockSpec((1, k), lambda i, j: (0, 0)),
            pl.BlockSpec((k, tn), lambda i, j: (0, j)),
        ],
        out_specs=pl.BlockSpec((tm, tn), lambda i, j: (i, j)),
        out_shape=jax.ShapeDtypeStruct((m, n), out_dtype),
        scratch_shapes=[pltpu.VMEM((tm, k), BF16)],
        compiler_params=_cparams("parallel", "arbitrary"),
        name="norm_matmul",
    )(x, g.reshape(1, k), w)


def _rmsnorm_kernel(x_ref, g_ref, o_ref):
    o_ref[...] = _rms(x_ref[...], g_ref[...])


def _rmsnorm(x, g, tm):
    m, k = x.shape
    return pl.pallas_call(
        _rmsnorm_kernel,
        grid=(m // tm,),
        in_specs=[pl.BlockSpec((tm, k), lambda i: (i, 0)), pl.BlockSpec((1, k), lambda i: (0, 0))],
        out_specs=pl.BlockSpec((tm, k), lambda i: (i, 0)),
        out_shape=jax.ShapeDtypeStruct((m, k), F32),
        compiler_params=_cparams("parallel"),
        name="final_rmsnorm",
    )(x, g.reshape(1, k))


def _sb_kernel(q_ref, k_ref, v_ref, o_ref, acc_ref, c_ref, *, t):
    i = pl.program_id(1)
    acc_ref[...] = jnp.zeros_like(acc_ref)
    c_ref[...] = jnp.zeros_like(c_ref)
    rows = lax.broadcasted_iota(I32, (t, t), 0)
    cols = lax.broadcasted_iota(I32, (t, t), 1)
    later = (rows > cols).astype(BF16)
    later2 = jnp.concatenate([later, later], axis=0)
    diff = cols - rows
    q = q_ref[0] * jnp.asarray(HEAD_DIM ** -0.5, BF16)

    def cond(state):
        j, cmax = state
        return jnp.logical_and(j >= 0, cmax > SB_DEAD)

    def body(state):
        j, _ = state
        ks = pl.multiple_of(j * t, t)
        causal = diff < (i - j) * t
        cm = jnp.full((t, 1), -jnp.inf, F32)
        for h in range(N_HEADS):
            lo, hi = h * HEAD_DIM, (h + 1) * HEAD_DIM
            kh = k_ref[0, pl.ds(ks, t), lo:hi]
            vh = v_ref[0, pl.ds(ks, t), lo:hi]
            z = lax.dot_general(q[:, lo:hi], kh, NT_DIMS, preferred_element_type=F32)
            sp = jnp.maximum(z, 0.0) + jnp.log1p(jnp.exp(-jnp.abs(z)))
            lg1m = jnp.where(causal, -sp, 0.0)
            l_hi = lg1m.astype(BF16)
            l_lo = (lg1m - l_hi.astype(F32)).astype(BF16)
            ain = jnp.dot(jnp.concatenate([l_hi, l_lo], axis=1), later2, preferred_element_type=F32)
            c_h = c_ref[:, h:h + 1]
            w = jnp.where(causal, jnp.exp(z - sp + c_h + ain), 0.0)
            acc_ref[h] += jnp.dot(w.astype(BF16), vh, preferred_element_type=F32)
            c_new = c_h + ain[:, 0:1] + lg1m[:, 0:1]
            c_ref[:, h:h + 1] = c_new
            cm = jnp.maximum(cm, c_new)
        return j - 1, jnp.max(cm)

    lax.while_loop(cond, body, (i, jnp.asarray(0.0, F32)))
    for h in range(N_HEADS):
        o_ref[0, :, h * HEAD_DIM:(h + 1) * HEAD_DIM] = acc_ref[h].astype(o_ref.dtype)


def _sb_attention(qkv, t):
    b, s, _ = qkv.shape
    return pl.pallas_call(
        functools.partial(_sb_kernel, t=t),
        grid=(b, s // t),
        in_specs=[
            pl.BlockSpec((1, t, WIDTH), lambda bi, i: (bi, i, 0)),
            pl.BlockSpec((1, s, WIDTH), lambda bi, i: (bi, 0, 1)),
            pl.BlockSpec((1, s, WIDTH), lambda bi, i: (bi, 0, 2)),
        ],
        out_specs=pl.BlockSpec((1, t, WIDTH), lambda bi, i: (bi, i, 0)),
        out_shape=jax.ShapeDtypeStruct((b, s, WIDTH), BF16),
        scratch_shapes=[pltpu.VMEM((N_HEADS, t, HEAD_DIM), F32), pltpu.VMEM((t, 128), F32)],
        compiler_params=_cparams("parallel", "arbitrary"),
        name="sb_attention",
    )(qkv, qkv, qkv)


def _dsa_kernel(q_ref, k_ref, v_ref, qi_ref, ki_ref, wi_ref, o_ref,
                key_ref, acc_ref, m_ref, l_ref, *, tq, tc, top_k):
    i = pl.program_id(1)
    nchunks = ((i + 1) * tq + tc - 1) // tc
    q_pos = i * tq + lax.broadcasted_iota(I32, (tq, 1), 0)
    col = lax.broadcasted_iota(I32, (1, tc), 1)
    scale = jnp.asarray(HEAD_DIM ** -0.5, BF16)

    qi = qi_ref[0] * scale
    wi = wi_ref[0][:, IDX_DIM:IDX_DIM + N_HEADS] * (N_HEADS ** -0.5)

    def score_chunk(c, carry):
        ks = pl.multiple_of(c * tc, tc)
        kc = ki_ref[0, pl.ds(ks, tc), :]
        sc = jnp.zeros((tq, tc), F32)
        for h in range(N_HEADS):
            lg = lax.dot_general(qi[:, h * IDX_DIM:(h + 1) * IDX_DIM], kc, NT_DIMS,
                                 preferred_element_type=F32)
            sc = sc + wi[:, h:h + 1] * jnp.maximum(lg, 0.0)
        sc = jnp.where(sc == 0.0, 0.0, sc)
        bits = pltpu.bitcast(sc, I32)
        skey = bits ^ ((bits >> 31) & 0x7FFFFFFF)
        key_ref[:, pl.ds(ks, tc)] = jnp.where(ks + col <= q_pos, skey, INT_MIN)
        return carry

    lax.fori_loop(0, nchunks, score_chunk, 0)

    def bit_step(b, t_u):
        cand = t_u | lax.shift_left(jnp.asarray(1, I32), 31 - b)
        cand_s = jnp.broadcast_to(cand ^ INT_MIN, (tq, 128))

        def count_chunk(c, cnt):
            ks = pl.multiple_of(c * tc, tc)
            for u in range(tc // 128):
                kk = key_ref[:, pl.ds(ks + u * 128, 128)]
                cnt = cnt + jnp.where(kk >= cand_s, 1.0, 0.0)
            return cnt

        cnt = lax.fori_loop(0, nchunks, count_chunk, jnp.zeros((tq, 128), F32))
        total = jnp.sum(cnt, axis=1, keepdims=True)
        return jnp.where(total >= top_k, cand, t_u)

    t_u = lax.fori_loop(0, 32, bit_step, jnp.zeros((tq, 1), I32))
    thr = jnp.maximum(t_u ^ INT_MIN, INT_MIN + 1)

    m_ref[...] = jnp.full_like(m_ref, NEG_BIG)
    l_ref[...] = jnp.zeros_like(l_ref)
    acc_ref[...] = jnp.zeros_like(acc_ref)
    q = q_ref[0] * scale

    def attend_chunk(c, carry):
        ks = pl.multiple_of(c * tc, tc)
        sel = key_ref[:, pl.ds(ks, tc)] >= thr
        for h in range(N_HEADS):
            lo, hi = h * HEAD_DIM, (h + 1) * HEAD_DIM
            kh = k_ref[0, pl.ds(ks, tc), lo:hi]
            vh = v_ref[0, pl.ds(ks, tc), lo:hi]
            lg = lax.dot_general(q[:, lo:hi], kh, NT_DIMS, preferred_element_type=F32)
            lg = jnp.where(sel, lg, NEG_BIG)
            m_old = m_ref[:, h:h + 1]
            m_new = jnp.maximum(m_old, jnp.max(lg, axis=1, keepdims=True))
            p = jnp.where(sel, jnp.exp(lg - m_new), 0.0)
            alpha = jnp.exp(m_old - m_new)
            l_ref[:, h:h + 1] = alpha * l_ref[:, h:h + 1] + jnp.sum(p, axis=1, keepdims=True)
            acc_ref[h] = alpha * acc_ref[h] + jnp.dot(p.astype(BF16), vh, preferred_element_type=F32)
            m_ref[:, h:h + 1] = m_new
        return carry

    lax.fori_loop(0, nchunks, attend_chunk, 0)
    for h in range(N_HEADS):
        o_ref[0, :, h * HEAD_DIM:(h + 1) * HEAD_DIM] = (acc_ref[h] / l_ref[:, h:h + 1]).astype(o_ref.dtype)


def _dsa_attention(qkv, k_idx, idx_kw, tq, tc):
    b, s, _ = qkv.shape
    top_k = min(TOPK_MAX, s // 4)
    return pl.pallas_call(
        functools.partial(_dsa_kernel, tq=tq, tc=tc, top_k=top_k),
        grid=(b, s // tq),
        in_specs=[
            pl.BlockSpec((1, tq, WIDTH), lambda bi, i: (bi, i, 3)),
            pl.BlockSpec((1, s, WIDTH), lambda bi, i: (bi, 0, 4)),
            pl.BlockSpec((1, s, WIDTH), lambda bi, i: (bi, 0, 5)),
            pl.BlockSpec((1, tq, WIDTH), lambda bi, i: (bi, i, 6)),
            pl.BlockSpec((1, s, IDX_DIM), lambda bi, i: (bi, 0, 0)),
            pl.BlockSpec((1, tq, 128), lambda bi, i: (bi, i, 0)),
        ],
        out_specs=pl.BlockSpec((1, tq, WIDTH), lambda bi, i: (bi, i, 0)),
        out_shape=jax.ShapeDtypeStruct((b, s, WIDTH), BF16),
        scratch_shapes=[
            pltpu.VMEM((tq, s), I32),
            pltpu.VMEM((N_HEADS, tq, HEAD_DIM), F32),
            pltpu.VMEM((tq, 128), F32),
            pltpu.VMEM((tq, 128), F32),
        ],
        compiler_params=_cparams("parallel", "arbitrary"),
        name="dsa_attention",
    )(qkv, qkv, qkv, qkv, k_idx, idx_kw)


def _merge_kernel(x_ref, ya_ref, yc_ref, b_ref, c_ref, xi_ref, ch_ref, xh_ref,
                  g0_ref, g1_ref, g2_ref, cw_ref, wa_ref, wb_ref, wc_ref, wo_ref,
                  o_ref, u_ref, *, tm, seq):
    i = pl.program_id(0)
    halo_ok = ((i * tm) % seq != 0).astype(F32)
    u_ref[0:HALO, :] = ch_ref[...] * xh_ref[...] * halo_ok
    u_ref[HALO:, :] = c_ref[...] * xi_ref[...]
    cw = cw_ref[...]
    conv = (cw[0:1] * u_ref[HALO - 2:HALO - 2 + tm, :] + cw[1:2] * u_ref[HALO - 1:HALO - 1 + tm, :]
            + cw[2:3] * u_ref[HALO:, :])
    yb = (b_ref[...] * conv).astype(BF16)
    pa = jnp.dot(ya_ref[...], wa_ref[...], preferred_element_type=F32)
    pb = jnp.dot(yb, wb_ref[...], preferred_element_type=F32)
    pc = jnp.dot(yc_ref[...], wc_ref[...], preferred_element_type=F32)
    merged = (jax.nn.sigmoid(g0_ref[...]) * pa + jax.nn.sigmoid(g1_ref[...]) * pb
              + jax.nn.sigmoid(g2_ref[...]) * pc)
    o_ref[...] = x_ref[...] + jnp.dot(merged.astype(BF16), wo_ref[...], preferred_element_type=F32)


def _merge(x, ya, yc, zf, conv_w, wa, wb, wc, wo, tm, seq):
    m = x.shape[0]
    hb = tm // HALO
    row = lambda c: (lambda i: (i, c))
    halo = lambda c: (lambda i: (jnp.maximum(i * hb - 1, 0), c))
    const = lambda i: (0, 0)
    return pl.pallas_call(
        functools.partial(_merge_kernel, tm=tm, seq=seq),
        grid=(m // tm,),
        in_specs=[
            pl.BlockSpec((tm, D_MODEL), row(0)),
            pl.BlockSpec((tm, WIDTH), row(0)),
            pl.BlockSpec((tm, WIDTH), row(0)),
            pl.BlockSpec((tm, WIDTH), row(6)),
            pl.BlockSpec((tm, WIDTH), row(7)),
            pl.BlockSpec((tm, WIDTH), row(8)),
            pl.BlockSpec((HALO, WIDTH), halo(7)),
            pl.BlockSpec((HALO, WIDTH), halo(8)),
            pl.BlockSpec((tm, D_MODEL), row(0)),
            pl.BlockSpec((tm, D_MODEL), row(1)),
            pl.BlockSpec((tm, D_MODEL), row(2)),
            pl.BlockSpec((3, WIDTH), const),
            pl.BlockSpec((WIDTH, D_MODEL), const),
            pl.BlockSpec((WIDTH, D_MODEL), const),
            pl.BlockSpec((WIDTH, D_MODEL), const),
            pl.BlockSpec((D_MODEL, D_MODEL), const),
        ],
        out_specs=pl.BlockSpec((tm, D_MODEL), row(0)),
        out_shape=jax.ShapeDtypeStruct((m, D_MODEL), F32),
        scratch_shapes=[pltpu.VMEM((tm + HALO, WIDTH), F32)],
        compiler_params=_cparams("parallel"),
        name="merge",
    )(x, ya, yc, zf, zf, zf, zf, zf, zf, zf, zf, conv_w, wa, wb, wc, wo)


def _xa_kernel(x_ref, g_ref, wq_ref, k_ref, v_ref, wo_ref, o_ref):
    x = x_ref[...]
    h = _rms(x, g_ref[...]).astype(BF16)
    q = jnp.dot(h, wq_ref[...], preferred_element_type=F32) * (XA_HEAD_DIM ** -0.5)
    q = q.astype(BF16)
    outs = []
    for hh in range(XA_HEADS):
        lo, hi = hh * XA_HEAD_DIM, (hh + 1) * XA_HEAD_DIM
        lg = lax.dot_general(q[:, lo:hi], k_ref[0, :, lo:hi], NT_DIMS, preferred_element_type=F32)
        p = jnp.exp(lg - jnp.max(lg, axis=1, keepdims=True))
        p = p / jnp.sum(p, axis=1, keepdims=True)
        outs.append(jnp.dot(p.astype(BF16), v_ref[0, :, lo:hi], preferred_element_type=F32))
    o = jnp.concatenate(outs, axis=1).astype(BF16)
    o_ref[...] = x + jnp.dot(o, wo_ref[...], preferred_element_type=F32)


def _cross_attention(x, g, wq, kv, wo, tm, seq):
    m = x.shape[0]
    per_seq = seq // tm
    return pl.pallas_call(
        _xa_kernel,
        grid=(m // tm,),
        in_specs=[
            pl.BlockSpec((tm, D_MODEL), lambda i: (i, 0)),
            pl.BlockSpec((1, D_MODEL), lambda i: (0, 0)),
            pl.BlockSpec((D_MODEL, D_MODEL), lambda i: (0, 0)),
            pl.BlockSpec((1, MEM_LEN, D_MODEL), lambda i: (i // per_seq, 0, 0)),
            pl.BlockSpec((1, MEM_LEN, D_MODEL), lambda i: (i // per_seq, 0, 1)),
            pl.BlockSpec((D_MODEL, D_MODEL), lambda i: (0, 0)),
        ],
        out_specs=pl.BlockSpec((tm, D_MODEL), lambda i: (i, 0)),
        out_shape=jax.ShapeDtypeStruct((m, D_MODEL), F32),
        compiler_params=_cparams("parallel"),
        name="cross_attention",
    )(x, g.reshape(1, D_MODEL), wq, kv, kv, wo)


def _ffn_kernel(x_ref, xh_ref, g_ref, wa_ref, wu_ref, cw_ref, cb_ref, wd_ref, o_ref,
                h_ref, hh_ref, a_ref, acc_ref, *, tm, seq):
    i = pl.program_id(0)
    j = pl.program_id(1)

    @pl.when(j == 0)
    def _():
        h_ref[...] = _rms(x_ref[...], g_ref[...]).astype(BF16)
        hh_ref[...] = _rms(xh_ref[...], g_ref[...]).astype(BF16)
        acc_ref[...] = jnp.zeros_like(acc_ref)

    halo_ok = ((i * tm) % seq != 0).astype(F32)
    a_ref[0:HALO, :] = jnp.dot(hh_ref[...], wa_ref[...], preferred_element_type=F32) * halo_ok
    a_ref[HALO:, :] = jnp.dot(h_ref[...], wa_ref[...], preferred_element_type=F32)
    u = jnp.dot(h_ref[...], wu_ref[...], preferred_element_type=F32)
    cw = cw_ref[...]
    a = (cw[0:1] * a_ref[HALO - 2:HALO - 2 + tm, :] + cw[1:2] * a_ref[HALO - 1:HALO - 1 + tm, :]
         + cw[2:3] * a_ref[HALO:, :] + cb_ref[...])
    act = jax.nn.gelu(a, approximate=True) * u
    acc_ref[...] += jnp.dot(act.astype(BF16), wd_ref[...], preferred_element_type=F32)

    @pl.when(j == pl.num_programs(1) - 1)
    def _():
        o_ref[...] = x_ref[...] + acc_ref[...]


def _conv_ffn(x, g, w_in, conv_w, conv_b, w_down, tm, tf, seq):
    m = x.shape[0]
    nf = D_FF // tf
    hb = tm // HALO
    return pl.pallas_call(
        functools.partial(_ffn_kernel, tm=tm, seq=seq),
        grid=(m // tm, nf),
        in_specs=[
            pl.BlockSpec((tm, D_MODEL), lambda i, j: (i, 0)),
            pl.BlockSpec((HALO, D_MODEL), lambda i, j: (jnp.maximum(i * hb - 1, 0), 0)),
            pl.BlockSpec((1, D_MODEL), lambda i, j: (0, 0)),
            pl.BlockSpec((D_MODEL, tf), lambda i, j: (0, j)),
            pl.BlockSpec((D_MODEL, tf), lambda i, j: (0, j + nf)),
            pl.BlockSpec((3, tf), lambda i, j: (0, j)),
            pl.BlockSpec((1, tf), lambda i, j: (0, j)),
            pl.BlockSpec((tf, D_MODEL), lambda i, j: (j, 0)),
        ],
        out_specs=pl.BlockSpec((tm, D_MODEL), lambda i, j: (i, 0)),
        out_shape=jax.ShapeDtypeStruct((m, D_MODEL), F32),
        scratch_shapes=[
            pltpu.VMEM((tm, D_MODEL), BF16),
            pltpu.VMEM((HALO, D_MODEL), BF16),
            pltpu.VMEM((tm + HALO, tf), F32),
            pltpu.VMEM((tm, D_MODEL), F32),
        ],
        compiler_params=_cparams("parallel", "arbitrary"),
        name="conv_ffn",
    )(x, x, g.reshape(1, D_MODEL), w_in, w_in, conv_w, conv_b.reshape(1, D_FF), w_down)


def kernel(x, mem, norm_mix, w_in, sc_conv_w, w_sb_out, w_sc_out, w_dsa_out, w_mix_o, norm_xa, norm_mem, w_xa_q, w_xa_kv, w_xa_o, norm_ffn, w_ffn_in, ffn_conv_w, ffn_conv_b, w_ffn_down, norm_final):
    batch, seq, _ = x.shape
    depth = w_in.shape[0]
    m = batch * seq
    xf = x.reshape(m, D_MODEL)
    memf = mem.reshape(batch * MEM_LEN, D_MODEL)
    o_sc, o_dsa, o_iq, o_ik, o_g = 3 * WIDTH, 6 * WIDTH, 9 * WIDTH, 10 * WIDTH, 10 * WIDTH + IDX_DIM + N_HEADS

    for l in range(depth):
        wl = w_in[l]
        w_qkv = jnp.concatenate([wl[:, :o_sc], wl[:, o_dsa:o_ik]], axis=1).astype(BF16)
        w_elt = jnp.concatenate([wl[:, o_g:], wl[:, o_sc:o_dsa]], axis=1).astype(BF16)
        w_ikw = jnp.pad(wl[:, o_ik:o_g], ((0, 0), (0, 128 - (o_g - o_ik)))).astype(BF16)

        qkv = _norm_matmul(xf, norm_mix[l], w_qkv, BF16, 512, 512).reshape(batch, seq, 7 * WIDTH)
        zf = _norm_matmul(xf, norm_mix[l], w_elt, F32, 512, 512)
        ikw = _norm_matmul(xf, norm_mix[l], w_ikw, F32, 512, 128).reshape(batch, seq, 128)
        k_idx = ikw[:, :, :IDX_DIM].astype(BF16)

        ya = _sb_attention(qkv, 256).reshape(m, WIDTH)
        yc = _dsa_attention(qkv, k_idx, ikw, 128, 256).reshape(m, WIDTH)
        xf = _merge(xf, ya, yc, zf, sc_conv_w[l], w_sb_out[l].astype(BF16), w_sc_out[l].astype(BF16),
                    w_dsa_out[l].astype(BF16), w_mix_o[l].astype(BF16), 512, seq)

        kv = _norm_matmul(memf, norm_mem[l], w_xa_kv[l].astype(BF16), BF16, 512, 512)
        xf = _cross_attention(xf, norm_xa[l], w_xa_q[l].astype(BF16), kv.reshape(batch, MEM_LEN, 2 * D_MODEL),
                              w_xa_o[l].astype(BF16), 512, seq)

        xf = _conv_ffn(xf, norm_ffn[l], w_ffn_in[l].astype(BF16), ffn_conv_w[l], ffn_conv_b[l],
                       w_ffn_down[l].astype(BF16), 512, 256, seq)

    return _rmsnorm(xf, norm_final, 512).reshape(batch, seq, D_MODEL)
```

```python
import functools

import jax
import jax.numpy as jnp
from jax import lax
from jax.experimental import pallas as pl
from jax.experimental.pallas import tpu as pltpu

F32 = jnp.float32
BF16 = jnp.bfloat16
I32 = jnp.int32

D_MODEL = 1024
HEAD_DIM = 64
N_HEADS = 8
WIDTH = N_HEADS * HEAD_DIM
IDX_DIM = 64
TOPK_MAX = 256
MEM_LEN = 256
XA_HEADS = 4
XA_HEAD_DIM = D_MODEL // XA_HEADS
D_FF = 2816
RMS_EPS = 1e-6
LANES = 128
HALO = 8
BF16_ROWS = 16
VT_ROWS = HEAD_DIM + BF16_ROWS
LOG2E = 1.4426950408889634
INT_MIN = -2 ** 31
SB_DEAD = -104.0
VMEM_LIMIT = 56 * 1024 * 1024

NT_DIMS = (((1,), (1,)), ((), ()))


def _cparams(*sem):
    return pltpu.CompilerParams(dimension_semantics=sem, vmem_limit_bytes=VMEM_LIMIT)


def _rms(x, g):
    ms = jnp.mean(x * x, axis=-1, keepdims=True)
    return x * lax.rsqrt(ms + RMS_EPS) * g


def _norm_matmul_kernel(x_ref, g_ref, w_ref, o_ref, h_ref):
    @pl.when(pl.program_id(1) == 0)
    def _():
        h_ref[...] = _rms(x_ref[...], g_ref[...]).astype(BF16)

    o_ref[...] = jnp.dot(h_ref[...], w_ref[...], preferred_element_type=F32).astype(o_ref.dtype)


def _norm_matmul(x, g, w, out_dtype, tm, tn):
    m, k = x.shape
    n = w.shape[1]
    return pl.pallas_call(
        _norm_matmul_kernel,
        grid=(m // tm, n // tn),
        in_specs=[
            pl.BlockSpec((tm, k), lambda i, j: (i, 0)),
            pl.BlockSpec((1, k), lambda i, j: (0, 0)),
            pl.BlockSpec((k, tn), lambda i, j: (0, j)),
        ],
        out_specs=pl.BlockSpec((tm, tn), lambda i, j: (i, j)),
        out_shape=jax.ShapeDtypeStruct((m, n), out_dtype),
        scratch_shapes=[pltpu.VMEM((tm, k), BF16)],
        compiler_params=_cparams("parallel", "arbitrary"),
        name="norm_matmul",
    )(x, g.reshape(1, k), w)


def _norm_matmul_t_kernel(x_ref, g_ref, wt_ref, b_ref, o_ref):
    h = _rms(x_ref[...], g_ref[...]).astype(BF16)
    z = lax.dot_general(wt_ref[...], h, NT_DIMS, preferred_element_type=F32)
    o_ref[...] = (z + b_ref[...]).astype(o_ref.dtype)


def _norm_matmul_t(x, g, wt, bias, out_dtype, tm):
    m, k = x.shape
    n = wt.shape[0]
    return pl.pallas_call(
        _norm_matmul_t_kernel,
        grid=(m // tm,),
        in_specs=[
            pl.BlockSpec((tm, k), lambda i: (i, 0)),
            pl.BlockSpec((1, k), lambda i: (0, 0)),
            pl.BlockSpec((n, k), lambda i: (0, 0)),
            pl.BlockSpec((n, 1), lambda i: (0, 0)),
        ],
        out_specs=pl.BlockSpec((n, tm), lambda i: (0, i)),
        out_shape=jax.ShapeDtypeStruct((n, m), out_dtype),
        compiler_params=_cparams("parallel"),
        name="norm_matmul_t",
    )(x, g.reshape(1, k), wt, bias)


def _rmsnorm_kernel(x_ref, g_ref, o_ref):
    o_ref[...] = _rms(x_ref[...], g_ref[...])


def _rmsnorm(x, g, tm):
    m, k = x.shape
    return pl.pallas_call(
        _rmsnorm_kernel,
        grid=(m // tm,),
        in_specs=[pl.BlockSpec((tm, k), lambda i: (i, 0)), pl.BlockSpec((1, k), lambda i: (0, 0))],
        out_specs=pl.BlockSpec((tm, k), lambda i: (i, 0)),
        out_shape=jax.ShapeDtypeStruct((m, k), F32),
        compiler_params=_cparams("parallel"),
        name="final_rmsnorm",
    )(x, g.reshape(1, k))


def _sb_kernel(q_ref, k_ref, v_ref, o_ref, acc_ref, c_ref, *, t):
    i = pl.program_id(1)
    acc_ref[...] = jnp.zeros_like(acc_ref)
    c_ref[...] = jnp.zeros_like(c_ref)
    rows = lax.broadcasted_iota(I32, (t, t), 0)
    cols = lax.broadcasted_iota(I32, (t, t), 1)
    later = (rows > cols).astype(BF16)
    later2 = jnp.concatenate([later, later], axis=0)
    diff = cols - rows
    q = q_ref[0]

    def cond(state):
        j, cmax = state
        return jnp.logical_and(j >= 0, cmax > SB_DEAD)

    def body(state):
        j, _ = state
        ks = pl.multiple_of(j * t, t)
        causal = diff < (i - j) * t
        cm = jnp.full((t, 1), -jnp.inf, F32)
        for h in range(N_HEADS):
            lo, hi = h * HEAD_DIM, (h + 1) * HEAD_DIM
            kh = k_ref[0, pl.ds(ks, t), lo:hi]
            vh = v_ref[0, pl.ds(ks, t), lo:hi]
            z = lax.dot_general(q[:, lo:hi], kh, NT_DIMS, preferred_element_type=F32)
            sp = jnp.maximum(z, 0.0) + jnp.log1p(jnp.exp(-jnp.abs(z)))
            lg1m = jnp.where(causal, -sp, 0.0)
            l_hi = lg1m.astype(BF16)
            l_lo = (lg1m - l_hi.astype(F32)).astype(BF16)
            ain = jnp.dot(jnp.concatenate([l_hi, l_lo], axis=1), later2, preferred_element_type=F32)
            c_h = c_ref[:, h:h + 1]
            w = jnp.where(causal, jnp.exp(z - sp + c_h + ain), 0.0)
            acc_ref[h] += jnp.dot(w.astype(BF16), vh, preferred_element_type=F32)
            c_new = c_h + ain[:, 0:1] + lg1m[:, 0:1]
            c_ref[:, h:h + 1] = c_new
            cm = jnp.maximum(cm, c_new)
        return j - 1, jnp.max(cm)

    lax.while_loop(cond, body, (i, jnp.asarray(0.0, F32)))
    for h in range(N_HEADS):
        o_ref[0, :, h * HEAD_DIM:(h + 1) * HEAD_DIM] = acc_ref[h].astype(o_ref.dtype)


def _sb_attention(qkv, t):
    b, s, _ = qkv.shape
    return pl.pallas_call(
        functools.partial(_sb_kernel, t=t),
        grid=(b, s // t),
        in_specs=[
            pl.BlockSpec((1, t, WIDTH), lambda bi, i: (bi, i, 0)),
            pl.BlockSpec((1, s, WIDTH), lambda bi, i: (bi, 0, 1)),
            pl.BlockSpec((1, s, WIDTH), lambda bi, i: (bi, 0, 2)),
        ],
        out_specs=pl.BlockSpec((1, t, WIDTH), lambda bi, i: (bi, i, 0)),
        out_shape=jax.ShapeDtypeStruct((b, s, WIDTH), BF16),
        scratch_shapes=[pltpu.VMEM((N_HEADS, t, HEAD_DIM), F32), pltpu.VMEM((t, LANES), F32)],
        compiler_params=_cparams("parallel", "arbitrary"),
        name="sb_attention",
    )(qkv, qkv, qkv)


def _dsa_kernel(q_ref, k_ref, vt_ref, qi_ref, ki_ref, wi_ref, o_ref,
                key_ref, bias_ref, lg_ref, p_ref, qx_ref, qih_ref, acc_ref, m_ref, alpha_ref, tie_ref,
                *, tq, tc, ta, top_k):
    i = pl.program_id(1)
    nchunks = (i + 1) * (tq // tc)
    q_pos = i * tq + lax.broadcasted_iota(I32, (1, tq), 1)
    row = lax.broadcasted_iota(I32, (tc, 1), 0)

    qi = qi_ref[0]
    qf = q_ref[0].astype(F32)
    upper = lax.broadcasted_iota(I32, (1, LANES), 1) >= HEAD_DIM
    for h in range(N_HEADS):
        qih_ref[h] = qi[:, h * IDX_DIM:(h + 1) * IDX_DIM]
        pair = qf[:, (h // 2) * LANES:(h // 2 + 1) * LANES]
        keep = upper if h % 2 else jnp.logical_not(upper)
        qx_ref[h] = jnp.where(keep, pair, 0.0).astype(BF16)
    wt = jnp.transpose(wi_ref[0])[IDX_DIM:IDX_DIM + N_HEADS, :] * (N_HEADS ** -0.5)

    def score_chunk(c, carry):
        ks = pl.multiple_of(c * tc, tc)
        kc = ki_ref[0, pl.ds(ks, tc), :]
        sc = jnp.zeros((tc, tq), F32)
        for h in range(N_HEADS):
            lg = lax.dot_general(kc, qih_ref[h], NT_DIMS, preferred_element_type=F32)
            sc = sc + wt[h:h + 1, :] * jnp.maximum(lg, 0.0)
        sc = jnp.where(sc == 0.0, 0.0, sc)
        bits = pltpu.bitcast(sc, I32)
        skey = bits ^ ((bits >> 31) & 0x7FFFFFFF)
        key_ref[pl.ds(ks, tc), :] = jnp.where(ks + row <= q_pos, skey, INT_MIN)
        return carry

    lax.fori_loop(0, nchunks, score_chunk, 0)

    slab = 32

    def count(pred):
        def count_chunk(c, cnt):
            ks = pl.multiple_of(c * tc, tc)
            for u in range(tc // slab):
                cnt = cnt + jnp.where(pred(key_ref[pl.ds(ks + u * slab, slab), :]), 1, 0)
            return cnt

        cnt = lax.fori_loop(0, nchunks, count_chunk, jnp.zeros((slab, tq), I32))
        return jnp.sum(cnt.astype(F32), axis=0, keepdims=True)

    def bit_step(b, t_u):
        cand = t_u | lax.shift_left(jnp.asarray(1, I32), 31 - b)
        cand_s = jnp.broadcast_to(cand ^ INT_MIN, (slab, tq))
        return jnp.where(count(lambda kk: kk >= cand_s) >= top_k, cand, t_u)

    t_u = lax.fori_loop(0, 32, bit_step, jnp.zeros((1, tq), I32))
    thr = jnp.maximum(t_u ^ INT_MIN, INT_MIN + 1)
    thr_b = jnp.broadcast_to(thr, (slab, tq))
    n_ge = count(lambda kk: kk >= thr_b)
    n_gt = count(lambda kk: kk > thr_b)
    has_ties = jnp.max(n_ge) > top_k
    quota = top_k - n_gt

    m_ref[...] = jnp.full_like(m_ref, -jnp.inf)
    acc_ref[...] = jnp.zeros_like(acc_ref)
    tie_ref[...] = jnp.zeros_like(tie_ref)

    def attend_chunk(c, carry):
        ks = pl.multiple_of(c * ta, ta)
        kk = key_ref[pl.ds(ks, ta), :]

        def plain_bias():
            return jnp.where(kk >= thr, 0.0, -jnp.inf)

        def tie_bias():
            tie = kk == thr
            before = (lax.broadcasted_iota(I32, (ta, ta), 1) <= lax.broadcasted_iota(I32, (ta, ta), 0))
            seen = tie_ref[...] + jnp.dot(before.astype(BF16), jnp.where(tie, 1.0, 0.0).astype(BF16),
                                          preferred_element_type=F32)
            tie_ref[...] = seen[ta - 1:ta, :]
            take = jnp.logical_or(kk > thr, jnp.logical_and(tie, seen <= quota))
            return jnp.where(take, 0.0, -jnp.inf)

        bias_ref[...] = lax.cond(has_ties, tie_bias, plain_bias)
        for h in range(N_HEADS):
            kp = k_ref[0, pl.ds(ks, ta), (h // 2) * LANES:(h // 2 + 1) * LANES]
            lg_ref[h] = lax.dot_general(kp, qx_ref[h], NT_DIMS, preferred_element_type=F32)
        part_max = [jnp.full((HALO, tq), -jnp.inf, F32) for _ in range(N_HEADS)]
        for r in range(0, ta, HALO):
            bias = bias_ref[r:r + HALO, :]
            for h in range(N_HEADS):
                lg = lg_ref[h, r:r + HALO, :] + bias
                lg_ref[h, r:r + HALO, :] = lg
                part_max[h] = jnp.maximum(part_max[h], lg)
        for h in range(N_HEADS):
            m_old = m_ref[h:h + 1, :]
            m_new = jnp.maximum(m_old, jnp.max(part_max[h], axis=0, keepdims=True))
            m_fin = jnp.where(m_new == -jnp.inf, 0.0, m_new)
            m_rows = jnp.broadcast_to(m_fin, (BF16_ROWS, tq))
            for r in range(0, ta, BF16_ROWS):
                p_ref[h, r:r + BF16_ROWS, :] = jnp.exp2(lg_ref[h, r:r + BF16_ROWS, :] - m_rows).astype(BF16)
            alpha_ref[h:h + 1, :] = jnp.exp2(m_old - m_fin)
            m_ref[h:h + 1, :] = m_new
        for h in range(N_HEADS):
            lo, hi = h * VT_ROWS, (h + 1) * VT_ROWS
            pv = jnp.dot(vt_ref[lo:hi, pl.ds(ks, ta)], p_ref[h], preferred_element_type=F32)
            acc_ref[lo:hi, :] = alpha_ref[h:h + 1, :] * acc_ref[lo:hi, :] + pv
        return carry

    lax.fori_loop(0, (i + 1) * (tq // ta), attend_chunk, 0)
    outs = []
    for h in range(N_HEADS):
        lo = h * VT_ROWS
        outs.append(acc_ref[lo:lo + HEAD_DIM, :] / acc_ref[lo + HEAD_DIM:lo + HEAD_DIM + 1, :])
    o_ref[0] = jnp.transpose(jnp.concatenate(outs, axis=0)).astype(o_ref.dtype)


def _dsa_attention(qkv, vt, k_idx, idx_kw, tq, tc, ta):
    b, s, _ = qkv.shape
    top_k = min(TOPK_MAX, s // 4)
    return pl.pallas_call(
        functools.partial(_dsa_kernel, tq=tq, tc=tc, ta=ta, top_k=top_k),
        grid=(b, s // tq),
        in_specs=[
            pl.BlockSpec((1, tq, WIDTH), lambda bi, i: (bi, i, 3)),
            pl.BlockSpec((1, s, WIDTH), lambda bi, i: (bi, 0, 4)),
            pl.BlockSpec((N_HEADS * VT_ROWS, s), lambda bi, i: (0, bi)),
            pl.BlockSpec((1, tq, WIDTH), lambda bi, i: (bi, i, 5)),
            pl.BlockSpec((1, s, IDX_DIM), lambda bi, i: (bi, 0, 0)),
            pl.BlockSpec((1, tq, LANES), lambda bi, i: (bi, i, 0)),
        ],
        out_specs=pl.BlockSpec((1, tq, WIDTH), lambda bi, i: (bi, i, 0)),
        out_shape=jax.ShapeDtypeStruct((b, s, WIDTH), BF16),
        scratch_shapes=[
            pltpu.VMEM((s, tq), I32),
            pltpu.VMEM((ta, tq), F32),
            pltpu.VMEM((N_HEADS, ta, tq), F32),
            pltpu.VMEM((N_HEADS, ta, tq), BF16),
            pltpu.VMEM((N_HEADS, tq, LANES), BF16),
            pltpu.VMEM((N_HEADS, tq, IDX_DIM), BF16),
            pltpu.VMEM((N_HEADS * VT_ROWS, tq), F32),
            pltpu.VMEM((N_HEADS, tq), F32),
            pltpu.VMEM((N_HEADS, tq), F32),
            pltpu.VMEM((1, tq), F32),
        ],
        compiler_params=_cparams("parallel", "arbitrary"),
        name="dsa_attention",
    )(qkv, qkv, vt, qkv, k_idx, idx_kw)


def _merge_kernel(x_ref, ya_ref, yc_ref, b_ref, c_ref, xi_ref, ch_ref, xh_ref,
                  g0_ref, g1_ref, g2_ref, cw_ref, wa_ref, wb_ref, wc_ref, wo_ref,
                  o_ref, u_ref, *, tm, seq):
    i = pl.program_id(0)
    halo_ok = ((i * tm) % seq != 0).astype(F32)
    u_ref[0:HALO, :] = ch_ref[...] * xh_ref[...] * halo_ok
    u_ref[HALO:, :] = c_ref[...] * xi_ref[...]
    cw = cw_ref[...]
    conv = (cw[0:1] * u_ref[HALO - 2:HALO - 2 + tm, :] + cw[1:2] * u_ref[HALO - 1:HALO - 1 + tm, :]
            + cw[2:3] * u_ref[HALO:, :])
    yb = (b_ref[...] * conv).astype(BF16)
    pa = jnp.dot(ya_ref[...], wa_ref[...], preferred_element_type=F32)
    pb = jnp.dot(yb, wb_ref[...], preferred_element_type=F32)
    pc = jnp.dot(yc_ref[...], wc_ref[...], preferred_element_type=F32)
    merged = (jax.nn.sigmoid(g0_ref[...]) * pa + jax.nn.sigmoid(g1_ref[...]) * pb
              + jax.nn.sigmoid(g2_ref[...]) * pc)
    o_ref[...] = x_ref[...] + jnp.dot(merged.astype(BF16), wo_ref[...], preferred_element_type=F32)


def _merge(x, ya, yc, zf, conv_w, wa, wb, wc, wo, tm, seq):
    m = x.shape[0]
    hb = tm // HALO
    row = lambda c: (lambda i: (i, c))
    halo = lambda c: (lambda i: (jnp.maximum(i * hb - 1, 0), c))
    const = lambda i: (0, 0)
    return pl.pallas_call(
        functools.partial(_merge_kernel, tm=tm, seq=seq),
        grid=(m // tm,),
        in_specs=[
            pl.BlockSpec((tm, D_MODEL), row(0)),
            pl.BlockSpec((tm, WIDTH), row(0)),
            pl.BlockSpec((tm, WIDTH), row(0)),
            pl.BlockSpec((tm, WIDTH), row(6)),
            pl.BlockSpec((tm, WIDTH), row(7)),
            pl.BlockSpec((tm, WIDTH), row(8)),
            pl.BlockSpec((HALO, WIDTH), halo(7)),
            pl.BlockSpec((HALO, WIDTH), halo(8)),
            pl.BlockSpec((tm, D_MODEL), row(0)),
            pl.BlockSpec((tm, D_MODEL), row(1)),
            pl.BlockSpec((tm, D_MODEL), row(2)),
            pl.BlockSpec((3, WIDTH), const),
            pl.BlockSpec((WIDTH, D_MODEL), const),
            pl.BlockSpec((WIDTH, D_MODEL), const),
            pl.BlockSpec((WIDTH, D_MODEL), const),
            pl.BlockSpec((D_MODEL, D_MODEL), const),
        ],
        out_specs=pl.BlockSpec((tm, D_MODEL), row(0)),
        out_shape=jax.ShapeDtypeStruct((m, D_MODEL), F32),
        scratch_shapes=[pltpu.VMEM((tm + HALO, WIDTH), F32)],
        compiler_params=_cparams("parallel"),
        name="merge",
    )(x, ya, yc, zf, zf, zf, zf, zf, zf, zf, zf, conv_w, wa, wb, wc, wo)


def _xa_kernel(x_ref, g_ref, wq_ref, k_ref, v_ref, wo_ref, o_ref):
    x = x_ref[...]
    h = _rms(x, g_ref[...]).astype(BF16)
    q = jnp.dot(h, wq_ref[...], preferred_element_type=F32) * (XA_HEAD_DIM ** -0.5)
    q = q.astype(BF16)
    outs = []
    for hh in range(XA_HEADS):
        lo, hi = hh * XA_HEAD_DIM, (hh + 1) * XA_HEAD_DIM
        lg = lax.dot_general(q[:, lo:hi], k_ref[0, :, lo:hi], NT_DIMS, preferred_element_type=F32)
        p = jnp.exp(lg - jnp.max(lg, axis=1, keepdims=True))
        p = p / jnp.sum(p, axis=1, keepdims=True)
        outs.append(jnp.dot(p.astype(BF16), v_ref[0, :, lo:hi], preferred_element_type=F32))
    o = jnp.concatenate(outs, axis=1).astype(BF16)
    o_ref[...] = x + jnp.dot(o, wo_ref[...], preferred_element_type=F32)


def _cross_attention(x, g, wq, kv, wo, tm, seq):
    m = x.shape[0]
    per_seq = seq // tm
    return pl.pallas_call(
        _xa_kernel,
        grid=(m // tm,),
        in_specs=[
            pl.BlockSpec((tm, D_MODEL), lambda i: (i, 0)),
            pl.BlockSpec((1, D_MODEL), lambda i: (0, 0)),
            pl.BlockSpec((D_MODEL, D_MODEL), lambda i: (0, 0)),
            pl.BlockSpec((1, MEM_LEN, D_MODEL), lambda i: (i // per_seq, 0, 0)),
            pl.BlockSpec((1, MEM_LEN, D_MODEL), lambda i: (i // per_seq, 0, 1)),
            pl.BlockSpec((D_MODEL, D_MODEL), lambda i: (0, 0)),
        ],
        out_specs=pl.BlockSpec((tm, D_MODEL), lambda i: (i, 0)),
        out_shape=jax.ShapeDtypeStruct((m, D_MODEL), F32),
        compiler_params=_cparams("parallel"),
        name="cross_attention",
    )(x, g.reshape(1, D_MODEL), wq, kv, kv, wo)


def _ffn_kernel(x_ref, xh_ref, g_ref, wa_ref, wu_ref, cw_ref, cb_ref, wd_ref, o_ref,
                h_ref, hh_ref, a_ref, acc_ref, *, tm, seq):
    i = pl.program_id(0)
    j = pl.program_id(1)

    @pl.when(j == 0)
    def _():
        h_ref[...] = _rms(x_ref[...], g_ref[...]).astype(BF16)
        hh_ref[...] = _rms(xh_ref[...], g_ref[...]).astype(BF16)
        acc_ref[...] = jnp.zeros_like(acc_ref)

    halo_ok = ((i * tm) % seq != 0).astype(F32)
    a_ref[0:HALO, :] = jnp.dot(hh_ref[...], wa_ref[...], preferred_element_type=F32) * halo_ok
    a_ref[HALO:, :] = jnp.dot(h_ref[...], wa_ref[...], preferred_element_type=F32)
    u = jnp.dot(h_ref[...], wu_ref[...], preferred_element_type=F32)
    cw = cw_ref[...]
    a = (cw[0:1] * a_ref[HALO - 2:HALO - 2 + tm, :] + cw[1:2] * a_ref[HALO - 1:HALO - 1 + tm, :]
         + cw[2:3] * a_ref[HALO:, :] + cb_ref[...])
    act = jax.nn.gelu(a, approximate=True) * u
    acc_ref[...] += jnp.dot(act.astype(BF16), wd_ref[...], preferred_element_type=F32)

    @pl.when(j == pl.num_programs(1) - 1)
    def _():
        o_ref[...] = x_ref[...] + acc_ref[...]


def _conv_ffn(x, g, w_in, conv_w, conv_b, w_down, tm, tf, seq):
    m = x.shape[0]
    nf = D_FF // tf
    hb = tm // HALO
    return pl.pallas_call(
        functools.partial(_ffn_kernel, tm=tm, seq=seq),
        grid=(m // tm, nf),
        in_specs=[
            pl.BlockSpec((tm, D_MODEL), lambda i, j: (i, 0)),
            pl.BlockSpec((HALO, D_MODEL), lambda i, j: (jnp.maximum(i * hb - 1, 0), 0)),
            pl.BlockSpec((1, D_MODEL), lambda i, j: (0, 0)),
            pl.BlockSpec((D_MODEL, tf), lambda i, j: (0, j)),
            pl.BlockSpec((D_MODEL, tf), lambda i, j: (0, j + nf)),
            pl.BlockSpec((3, tf), lambda i, j: (0, j)),
            pl.BlockSpec((1, tf), lambda i, j: (0, j)),
            pl.BlockSpec((tf, D_MODEL), lambda i, j: (j, 0)),
        ],
        out_specs=pl.BlockSpec((tm, D_MODEL), lambda i, j: (i, 0)),
        out_shape=jax.ShapeDtypeStruct((m, D_MODEL), F32),
        scratch_shapes=[
            pltpu.VMEM((tm, D_MODEL), BF16),
            pltpu.VMEM((HALO, D_MODEL), BF16),
            pltpu.VMEM((tm + HALO, tf), F32),
            pltpu.VMEM((tm, D_MODEL), F32),
        ],
        compiler_params=_cparams("parallel", "arbitrary"),
        name="conv_ffn",
    )(x, x, g.reshape(1, D_MODEL), w_in, w_in, conv_w, conv_b.reshape(1, D_FF), w_down)


def kernel(x, mem, norm_mix, w_in, sc_conv_w, w_sb_out, w_sc_out, w_dsa_out, w_mix_o, norm_xa, norm_mem, w_xa_q, w_xa_kv, w_xa_o, norm_ffn, w_ffn_in, ffn_conv_w, ffn_conv_b, w_ffn_down, norm_final):
    batch, seq, _ = x.shape
    depth = w_in.shape[0]
    m = batch * seq
    xf = x.reshape(m, D_MODEL)
    memf = mem.reshape(batch * MEM_LEN, D_MODEL)
    o_sc, o_dsa, o_iq, o_ik, o_g = 3 * WIDTH, 6 * WIDTH, 9 * WIDTH, 10 * WIDTH, 10 * WIDTH + IDX_DIM + N_HEADS

    for l in range(depth):
        wl = w_in[l]
        qs = HEAD_DIM ** -0.5
        w_qkv = jnp.concatenate([
            wl[:, :WIDTH] * qs, wl[:, WIDTH:o_sc],
            wl[:, o_dsa:o_dsa + WIDTH] * (qs * LOG2E), wl[:, o_dsa + WIDTH:o_dsa + 2 * WIDTH],
            wl[:, o_iq:o_ik] * IDX_DIM ** -0.5], axis=1).astype(BF16)
        w_v = wl[:, o_dsa + 2 * WIDTH:o_iq].T.reshape(N_HEADS, HEAD_DIM, D_MODEL)
        w_vt = jnp.pad(w_v, ((0, 0), (0, BF16_ROWS), (0, 0))).reshape(N_HEADS * VT_ROWS, D_MODEL).astype(BF16)
        ones_rows = jnp.tile(jnp.arange(VT_ROWS) >= HEAD_DIM, N_HEADS).astype(F32).reshape(N_HEADS * VT_ROWS, 1)
        w_elt = jnp.concatenate([wl[:, o_g:], wl[:, o_sc:o_dsa]], axis=1).astype(BF16)
        w_ikw = jnp.pad(wl[:, o_ik:o_g], ((0, 0), (0, LANES - (o_g - o_ik)))).astype(BF16)

        qkv = _norm_matmul(xf, norm_mix[l], w_qkv, BF16, 512, 512).reshape(batch, seq, 6 * WIDTH)
        vt = _norm_matmul_t(xf, norm_mix[l], w_vt, ones_rows, BF16, 512)
        zf = _norm_matmul(xf, norm_mix[l], w_elt, F32, 512, 512)
        ikw = _norm_matmul(xf, norm_mix[l], w_ikw, F32, 512, LANES).reshape(batch, seq, LANES)
        k_idx = ikw[:, :, :IDX_DIM].astype(BF16)

        ya = _sb_attention(qkv, 256).reshape(m, WIDTH)
        yc = _dsa_attention(qkv, vt, k_idx, ikw, 256, 128, 256).reshape(m, WIDTH)
        xf = _merge(xf, ya, yc, zf, sc_conv_w[l], w_sb_out[l].astype(BF16), w_sc_out[l].astype(BF16),
                    w_dsa_out[l].astype(BF16), w_mix_o[l].astype(BF16), 512, seq)

        kv = _norm_matmul(memf, norm_mem[l], w_xa_kv[l].astype(BF16), BF16, 512, 512)
        xf = _cross_attention(xf, norm_xa[l], w_xa_q[l].astype(BF16), kv.reshape(batch, MEM_LEN, 2 * D_MODEL),
                              w_xa_o[l].astype(BF16), 512, seq)

        xf = _conv_ffn(xf, norm_ffn[l], w_ffn_in[l].astype(BF16), ffn_conv_w[l], ffn_conv_b[l],
                       w_ffn_down[l].astype(BF16), 512, 256, seq)

    return _rmsnorm(xf, norm_final, 512).reshape(batch, seq, D_MODEL)
```

```python
import functools

import jax
import jax.numpy as jnp
from jax import lax
from jax.experimental import pallas as pl
from jax.experimental.pallas import tpu as pltpu

F32 = jnp.float32
BF16 = jnp.bfloat16
I32 = jnp.int32

D_MODEL = 1024
HEAD_DIM = 64
N_HEADS = 8
WIDTH = N_HEADS * HEAD_DIM
IDX_DIM = 64
TOPK_MAX = 256
MEM_LEN = 256
XA_HEADS = 4
XA_HEAD_DIM = D_MODEL // XA_HEADS
D_FF = 2816
RMS_EPS = 1e-6
LANES = 128
HALO = 8
BF16_ROWS = 16
VT_ROWS = HEAD_DIM + BF16_ROWS
LOG2E = 1.4426950408889634
INT_MIN = -2 ** 31
SB_DEAD = -104.0
VMEM_LIMIT = 56 * 1024 * 1024

NT_DIMS = (((1,), (1,)), ((), ()))


def _cparams(*sem):
    return pltpu.CompilerParams(dimension_semantics=sem, vmem_limit_bytes=VMEM_LIMIT)


def _rms(x, g):
    ms = jnp.mean(x * x, axis=-1, keepdims=True)
    return x * lax.rsqrt(ms + RMS_EPS) * g


def _norm_matmul_kernel(x_ref, g_ref, w_ref, o_ref, h_ref):
    @pl.when(pl.program_id(1) == 0)
    def _():
        h_ref[...] = _rms(x_ref[...], g_ref[...]).astype(BF16)

    o_ref[...] = jnp.dot(h_ref[...], w_ref[...], preferred_element_type=F32).astype(o_ref.dtype)


def _norm_matmul(x, g, w, out_dtype, tm, tn):
    m, k = x.shape
    n = w.shape[1]
    return pl.pallas_call(
        _norm_matmul_kernel,
        grid=(m // tm, n // tn),
        in_specs=[
            pl.BlockSpec((tm, k), lambda i, j: (i, 0)),
            pl.BlockSpec((1, k), lambda i, j: (0, 0)),
            pl.BlockSpec((k, tn), lambda i, j: (0, j)),
        ],
        out_specs=pl.BlockSpec((tm, tn), lambda i, j: (i, j)),
        out_shape=jax.ShapeDtypeStruct((m, n), out_dtype),
        scratch_shapes=[pltpu.VMEM((tm, k), BF16)],
        compiler_params=_cparams("parallel", "arbitrary"),
        name="norm_matmul",
    )(x, g.reshape(1, k), w)


def _resident(shape):
    return pl.BlockSpec(shape, lambda i: (0,) * len(shape), pipeline_mode=pl.Buffered(1))


def _mixer_proj_kernel(x_ref, g_ref, wq_ref, we_ref, wvt_ref, vb_ref, wik_ref,
                       qkv_ref, zf_ref, vt_ref, ikw_ref, h_ref, *, cn):
    h_ref[...] = _rms(x_ref[...], g_ref[...]).astype(BF16)
    for c in range(0, wq_ref.shape[1], cn):
        qkv_ref[:, c:c + cn] = jnp.dot(h_ref[...], wq_ref[:, c:c + cn],
                                       preferred_element_type=F32).astype(qkv_ref.dtype)
    for c in range(0, we_ref.shape[1], cn):
        zf_ref[:, c:c + cn] = jnp.dot(h_ref[...], we_ref[:, c:c + cn], preferred_element_type=F32)
    vt = lax.dot_general(wvt_ref[...], h_ref[...], NT_DIMS, preferred_element_type=F32)
    vt_ref[...] = (vt + vb_ref[...]).astype(vt_ref.dtype)
    ikw_ref[...] = jnp.dot(h_ref[...], wik_ref[...], preferred_element_type=F32)


def _mixer_proj(x, g, w_qkv, w_elt, w_vt, vt_bias, w_ikw, tm, cn):
    m, k = x.shape
    nq, ne, nv, ni = w_qkv.shape[1], w_elt.shape[1], w_vt.shape[0], w_ikw.shape[1]
    return pl.pallas_call(
        functools.partial(_mixer_proj_kernel, cn=cn),
        grid=(m // tm,),
        in_specs=[
            pl.BlockSpec((tm, k), lambda i: (i, 0)),
            _resident((1, k)), _resident((k, nq)), _resident((k, ne)), _resident((nv, k)),
            _resident((nv, 1)), _resident((k, ni)),
        ],
        out_specs=[
            pl.BlockSpec((tm, nq), lambda i: (i, 0)),
            pl.BlockSpec((tm, ne), lambda i: (i, 0)),
            pl.BlockSpec((nv, tm), lambda i: (0, i)),
            pl.BlockSpec((tm, ni), lambda i: (i, 0)),
        ],
        out_shape=[
            jax.ShapeDtypeStruct((m, nq), BF16),
            jax.ShapeDtypeStruct((m, ne), F32),
            jax.ShapeDtypeStruct((nv, m), BF16),
            jax.ShapeDtypeStruct((m, ni), F32),
        ],
        scratch_shapes=[pltpu.VMEM((tm, k), BF16)],
        compiler_params=_cparams("parallel"),
        name="mixer_proj",
    )(x, g.reshape(1, k), w_qkv, w_elt, w_vt, vt_bias, w_ikw)


def _rmsnorm_kernel(x_ref, g_ref, o_ref):
    o_ref[...] = _rms(x_ref[...], g_ref[...])


def _rmsnorm(x, g, tm):
    m, k = x.shape
    return pl.pallas_call(
        _rmsnorm_kernel,
        grid=(m // tm,),
        in_specs=[pl.BlockSpec((tm, k), lambda i: (i, 0)), pl.BlockSpec((1, k), lambda i: (0, 0))],
        out_specs=pl.BlockSpec((tm, k), lambda i: (i, 0)),
        out_shape=jax.ShapeDtypeStruct((m, k), F32),
        compiler_params=_cparams("parallel"),
        name="final_rmsnorm",
    )(x, g.reshape(1, k))


def _sb_kernel(q_ref, k_ref, v_ref, o_ref, acc_ref, c_ref, qx_ref, *, t):
    i = pl.program_id(1)
    acc_ref[...] = jnp.zeros_like(acc_ref)
    c_ref[...] = jnp.zeros_like(c_ref)
    rows = lax.broadcasted_iota(I32, (t, t), 0)
    cols = lax.broadcasted_iota(I32, (t, t), 1)
    later = (rows > cols).astype(BF16)
    sums = jnp.concatenate([later, jnp.ones((t, LANES), BF16)], axis=1)
    sums2 = jnp.concatenate([sums, sums], axis=0)
    diff = cols - rows
    qf = q_ref[0].astype(F32)
    upper = lax.broadcasted_iota(I32, (1, LANES), 1) >= HEAD_DIM
    for h in range(N_HEADS):
        pair = qf[:, (h // 2) * LANES:(h // 2 + 1) * LANES]
        keep = upper if h % 2 else jnp.logical_not(upper)
        qx_ref[h] = jnp.where(keep, pair, 0.0).astype(BF16)

    def cond(state):
        j, cmax = state
        return jnp.logical_and(j >= 0, cmax > SB_DEAD)

    def body(state):
        j, _ = state
        ks = pl.multiple_of(j * t, t)
        causal = diff < (i - j) * t
        cm = jnp.full((t, LANES), -jnp.inf, F32)
        for h in range(N_HEADS):
            lo, hi = (h // 2) * LANES, (h // 2 + 1) * LANES
            z = lax.dot_general(qx_ref[h], k_ref[0, pl.ds(ks, t), lo:hi], NT_DIMS, preferred_element_type=F32)
            sp = jnp.maximum(z, 0.0) + jnp.log(1.0 + jnp.exp(-jnp.abs(z)))
            lg1m = jnp.where(causal, -sp, 0.0)
            l_hi = lg1m.astype(BF16)
            l_lo = (lg1m - l_hi.astype(F32)).astype(BF16)
            s = jnp.dot(jnp.concatenate([l_hi, l_lo], axis=1), sums2, preferred_element_type=F32)
            c_h = c_ref[h]
            c_tile = jnp.concatenate([c_h] * (t // LANES), axis=1)
            w = jnp.where(causal, jnp.exp(z - sp + c_tile + s[:, :t]), 0.0)
            acc_ref[h] += jnp.dot(w.astype(BF16), v_ref[0, pl.ds(ks, t), lo:hi], preferred_element_type=F32)
            c_new = c_h + s[:, t:]
            c_ref[h] = c_new
            cm = jnp.maximum(cm, c_new)
        return j - 1, jnp.max(cm)

    lax.while_loop(cond, body, (i, jnp.asarray(0.0, F32)))
    for p in range(N_HEADS // 2):
        both = jnp.where(upper, acc_ref[2 * p + 1], acc_ref[2 * p])
        o_ref[0, :, p * LANES:(p + 1) * LANES] = both.astype(o_ref.dtype)


def _sb_attention(qkv, t):
    b, s, _ = qkv.shape
    return pl.pallas_call(
        functools.partial(_sb_kernel, t=t),
        grid=(b, s // t),
        in_specs=[
            pl.BlockSpec((1, t, WIDTH), lambda bi, i: (bi, i, 0)),
            pl.BlockSpec((1, s, WIDTH), lambda bi, i: (bi, 0, 1)),
            pl.BlockSpec((1, s, WIDTH), lambda bi, i: (bi, 0, 2)),
        ],
        out_specs=pl.BlockSpec((1, t, WIDTH), lambda bi, i: (bi, i, 0)),
        out_shape=jax.ShapeDtypeStruct((b, s, WIDTH), BF16),
        scratch_shapes=[
            pltpu.VMEM((N_HEADS, t, LANES), F32),
            pltpu.VMEM((N_HEADS, t, LANES), F32),
            pltpu.VMEM((N_HEADS, t, LANES), BF16),
        ],
        compiler_params=_cparams("parallel", "arbitrary"),
        name="sb_attention",
    )(qkv, qkv, qkv)


def _dsa_kernel(q_ref, k_ref, vt_ref, qi_ref, ki_ref, wi_ref, o_ref,
                key_ref, bias_ref, lg_ref, p_ref, qx_ref, qih_ref, acc_ref, m_ref, alpha_ref, tie_ref,
                *, tq, tc, ta, top_k):
    i = pl.program_id(1)
    nchunks = (i + 1) * (tq // tc)
    q_pos = i * tq + lax.broadcasted_iota(I32, (1, tq), 1)
    row = lax.broadcasted_iota(I32, (tc, 1), 0)

    qi = qi_ref[0]
    qf = q_ref[0].astype(F32)
    upper = lax.broadcasted_iota(I32, (1, LANES), 1) >= HEAD_DIM
    for h in range(N_HEADS):
        qih_ref[h] = qi[:, h * IDX_DIM:(h + 1) * IDX_DIM]
        pair = qf[:, (h // 2) * LANES:(h // 2 + 1) * LANES]
        keep = upper if h % 2 else jnp.logical_not(upper)
        qx_ref[h] = jnp.where(keep, pair, 0.0).astype(BF16)
    wt = jnp.transpose(wi_ref[0])[IDX_DIM:IDX_DIM + N_HEADS, :] * (N_HEADS ** -0.5)

    def score_chunk(c, carry):
        ks = pl.multiple_of(c * tc, tc)
        kc = ki_ref[0, pl.ds(ks, tc), :]
        sc = jnp.zeros((tc, tq), F32)
        for h in range(N_HEADS):
            lg = lax.dot_general(kc, qih_ref[h], NT_DIMS, preferred_element_type=F32)
            sc = sc + wt[h:h + 1, :] * jnp.maximum(lg, 0.0)
        sc = jnp.where(sc == 0.0, 0.0, sc)
        bits = pltpu.bitcast(sc, I32)
        skey = bits ^ ((bits >> 31) & 0x7FFFFFFF)
        key_ref[pl.ds(ks, tc), :] = jnp.where(ks + row <= q_pos, skey, INT_MIN)
        return carry

    lax.fori_loop(0, nchunks, score_chunk, 0)

    slab = 32

    def count(pred):
        def count_chunk(c, cnt):
            ks = pl.multiple_of(c * tc, tc)
            for u in range(tc // slab):
                cnt = cnt + jnp.where(pred(key_ref[pl.ds(ks + u * slab, slab), :]), 1, 0)
            return cnt

        cnt = lax.fori_loop(0, nchunks, count_chunk, jnp.zeros((slab, tq), I32))
        return jnp.sum(cnt.astype(F32), axis=0, keepdims=True)

    def bit_step(b, t_u):
        cand = t_u | lax.shift_left(jnp.asarray(1, I32), 31 - b)
        cand_s = jnp.broadcast_to(cand ^ INT_MIN, (slab, tq))
        return jnp.where(count(lambda kk: kk >= cand_s) >= top_k, cand, t_u)

    t_u = lax.fori_loop(0, 32, bit_step, jnp.zeros((1, tq), I32))
    thr = jnp.maximum(t_u ^ INT_MIN, INT_MIN + 1)
    thr_b = jnp.broadcast_to(thr, (slab, tq))
    n_ge = count(lambda kk: kk >= thr_b)
    n_gt = count(lambda kk: kk > thr_b)
    has_ties = jnp.max(n_ge) > top_k
    quota = top_k - n_gt

    m_ref[...] = jnp.full_like(m_ref, -jnp.inf)
    acc_ref[...] = jnp.zeros_like(acc_ref)
    tie_ref[...] = jnp.zeros_like(tie_ref)

    def attend_chunk(c, carry):
        ks = pl.multiple_of(c * ta, ta)
        kk = key_ref[pl.ds(ks, ta), :]

        def plain_bias():
            return jnp.where(kk >= thr, 0.0, -jnp.inf)

        def tie_bias():
            tie = kk == thr
            before = (lax.broadcasted_iota(I32, (ta, ta), 1) <= lax.broadcasted_iota(I32, (ta, ta), 0))
            seen = tie_ref[...] + jnp.dot(before.astype(BF16), jnp.where(tie, 1.0, 0.0).astype(BF16),
                                          preferred_element_type=F32)
            tie_ref[...] = seen[ta - 1:ta, :]
            take = jnp.logical_or(kk > thr, jnp.logical_and(tie, seen <= quota))
            return jnp.where(take, 0.0, -jnp.inf)

        bias_ref[...] = lax.cond(has_ties, tie_bias, plain_bias)
        for h in range(N_HEADS):
            kp = k_ref[0, pl.ds(ks, ta), (h // 2) * LANES:(h // 2 + 1) * LANES]
            lg_ref[h] = lax.dot_general(kp, qx_ref[h], NT_DIMS, preferred_element_type=F32)
        part_max = [jnp.full((HALO, tq), -jnp.inf, F32) for _ in range(N_HEADS)]
        for r in range(0, ta, HALO):
            bias = bias_ref[r:r + HALO, :]
            for h in range(N_HEADS):
                lg = lg_ref[h, r:r + HALO, :] + bias
                lg_ref[h, r:r + HALO, :] = lg
                part_max[h] = jnp.maximum(part_max[h], lg)
        for h in range(N_HEADS):
            m_old = m_ref[h:h + 1, :]
            m_new = jnp.maximum(m_old, jnp.max(part_max[h], axis=0, keepdims=True))
            m_fin = jnp.where(m_new == -jnp.inf, 0.0, m_new)
            m_rows = jnp.broadcast_to(m_fin, (BF16_ROWS, tq))
            for r in range(0, ta, BF16_ROWS):
                p_ref[h, r:r + BF16_ROWS, :] = jnp.exp2(lg_ref[h, r:r + BF16_ROWS, :] - m_rows).astype(BF16)
            alpha_ref[h:h + 1, :] = jnp.exp2(m_old - m_fin)
            m_ref[h:h + 1, :] = m_new
        for h in range(N_HEADS):
            lo, hi = h * VT_ROWS, (h + 1) * VT_ROWS
            pv = jnp.dot(vt_ref[lo:hi, pl.ds(ks, ta)], p_ref[h], preferred_element_type=F32)
            acc_ref[lo:hi, :] = alpha_ref[h:h + 1, :] * acc_ref[lo:hi, :] + pv
        return carry

    lax.fori_loop(0, (i + 1) * (tq // ta), attend_chunk, 0)
    outs = []
    for h in range(N_HEADS):
        lo = h * VT_ROWS
        outs.append(acc_ref[lo:lo + HEAD_DIM, :] / acc_ref[lo + HEAD_DIM:lo + HEAD_DIM + 1, :])
    o_ref[0] = jnp.transpose(jnp.concatenate(outs, axis=0)).astype(o_ref.dtype)


def _dsa_attention(qkv, vt, k_idx, idx_kw, tq, tc, ta):
    b, s, _ = qkv.shape
    top_k = min(TOPK_MAX, s // 4)
    return pl.pallas_call(
        functools.partial(_dsa_kernel, tq=tq, tc=tc, ta=ta, top_k=top_k),
        grid=(b, s // tq),
        in_specs=[
            pl.BlockSpec((1, tq, WIDTH), lambda bi, i: (bi, i, 3)),
            pl.BlockSpec((1, s, WIDTH), lambda bi, i: (bi, 0, 4)),
            pl.BlockSpec((N_HEADS * VT_ROWS, s), lambda bi, i: (0, bi)),
            pl.BlockSpec((1, tq, WIDTH), lambda bi, i: (bi, i, 5)),
            pl.BlockSpec((1, s, IDX_DIM), lambda bi, i: (bi, 0, 0)),
            pl.BlockSpec((1, tq, LANES), lambda bi, i: (bi, i, 0)),
        ],
        out_specs=pl.BlockSpec((1, tq, WIDTH), lambda bi, i: (bi, i, 0)),
        out_shape=jax.ShapeDtypeStruct((b, s, WIDTH), BF16),
        scratch_shapes=[
            pltpu.VMEM((s, tq), I32),
            pltpu.VMEM((ta, tq), F32),
            pltpu.VMEM((N_HEADS, ta, tq), F32),
            pltpu.VMEM((N_HEADS, ta, tq), BF16),
            pltpu.VMEM((N_HEADS, tq, LANES), BF16),
            pltpu.VMEM((N_HEADS, tq, IDX_DIM), BF16),
            pltpu.VMEM((N_HEADS * VT_ROWS, tq), F32),
            pltpu.VMEM((N_HEADS, tq), F32),
            pltpu.VMEM((N_HEADS, tq), F32),
            pltpu.VMEM((1, tq), F32),
        ],
        compiler_params=_cparams("parallel", "arbitrary"),
        name="dsa_attention",
    )(qkv, qkv, vt, qkv, k_idx, idx_kw)


def _merge_kernel(x_ref, ya_ref, yc_ref, b_ref, c_ref, xi_ref, ch_ref, xh_ref,
                  g0_ref, g1_ref, g2_ref, cw_ref, wa_ref, wb_ref, wc_ref, wo_ref,
                  o_ref, u_ref, *, tm, seq):
    i = pl.program_id(0)
    halo_ok = ((i * tm) % seq != 0).astype(F32)
    u_ref[0:HALO, :] = ch_ref[...] * xh_ref[...] * halo_ok
    u_ref[HALO:, :] = c_ref[...] * xi_ref[...]
    cw = cw_ref[...]
    conv = (cw[0:1] * u_ref[HALO - 2:HALO - 2 + tm, :] + cw[1:2] * u_ref[HALO - 1:HALO - 1 + tm, :]
            + cw[2:3] * u_ref[HALO:, :])
    yb = (b_ref[...] * conv).astype(BF16)
    pa = jnp.dot(ya_ref[...], wa_ref[...], preferred_element_type=F32)
    pb = jnp.dot(yb, wb_ref[...], preferred_element_type=F32)
    pc = jnp.dot(yc_ref[...], wc_ref[...], preferred_element_type=F32)
    merged = (jax.nn.sigmoid(g0_ref[...]) * pa + jax.nn.sigmoid(g1_ref[...]) * pb
              + jax.nn.sigmoid(g2_ref[...]) * pc)
    o_ref[...] = x_ref[...] + jnp.dot(merged.astype(BF16), wo_ref[...], preferred_element_type=F32)


def _merge(x, ya, yc, zf, conv_w, wa, wb, wc, wo, tm, seq):
    m = x.shape[0]
    hb = tm // HALO
    row = lambda c: (lambda i: (i, c))
    halo = lambda c: (lambda i: (jnp.maximum(i * hb - 1, 0), c))
    const = lambda i: (0, 0)
    return pl.pallas_call(
        functools.partial(_merge_kernel, tm=tm, seq=seq),
        grid=(m // tm,),
        in_specs=[
            pl.BlockSpec((tm, D_MODEL), row(0)),
            pl.BlockSpec((tm, WIDTH), row(0)),
            pl.BlockSpec((tm, WIDTH), row(0)),
            pl.BlockSpec((tm, WIDTH), row(6)),
            pl.BlockSpec((tm, WIDTH), row(7)),
            pl.BlockSpec((tm, WIDTH), row(8)),
            pl.BlockSpec((HALO, WIDTH), halo(7)),
            pl.BlockSpec((HALO, WIDTH), halo(8)),
            pl.BlockSpec((tm, D_MODEL), row(0)),
            pl.BlockSpec((tm, D_MODEL), row(1)),
            pl.BlockSpec((tm, D_MODEL), row(2)),
            pl.BlockSpec((3, WIDTH), const),
            pl.BlockSpec((WIDTH, D_MODEL), const),
            pl.BlockSpec((WIDTH, D_MODEL), const),
            pl.BlockSpec((WIDTH, D_MODEL), const),
            pl.BlockSpec((D_MODEL, D_MODEL), const),
        ],
        out_specs=pl.BlockSpec((tm, D_MODEL), row(0)),
        out_shape=jax.ShapeDtypeStruct((m, D_MODEL), F32),
        scratch_shapes=[pltpu.VMEM((tm + HALO, WIDTH), F32)],
        compiler_params=_cparams("parallel"),
        name="merge",
    )(x, ya, yc, zf, zf, zf, zf, zf, zf, zf, zf, conv_w, wa, wb, wc, wo)


def _xa_kernel(x_ref, g_ref, wq_ref, k_ref, v_ref, wo_ref, o_ref):
    x = x_ref[...]
    h = _rms(x, g_ref[...]).astype(BF16)
    q = jnp.dot(h, wq_ref[...], preferred_element_type=F32) * (XA_HEAD_DIM ** -0.5)
    q = q.astype(BF16)
    outs = []
    for hh in range(XA_HEADS):
        lo, hi = hh * XA_HEAD_DIM, (hh + 1) * XA_HEAD_DIM
        lg = lax.dot_general(q[:, lo:hi], k_ref[0, :, lo:hi], NT_DIMS, preferred_element_type=F32)
        p = jnp.exp(lg - jnp.max(lg, axis=1, keepdims=True))
        p = p / jnp.sum(p, axis=1, keepdims=True)
        outs.append(jnp.dot(p.astype(BF16), v_ref[0, :, lo:hi], preferred_element_type=F32))
    o = jnp.concatenate(outs, axis=1).astype(BF16)
    o_ref[...] = x + jnp.dot(o, wo_ref[...], preferred_element_type=F32)


def _cross_attention(x, g, wq, kv, wo, tm, seq):
    m = x.shape[0]
    per_seq = seq // tm
    return pl.pallas_call(
        _xa_kernel,
        grid=(m // tm,),
        in_specs=[
            pl.BlockSpec((tm, D_MODEL), lambda i: (i, 0)),
            pl.BlockSpec((1, D_MODEL), lambda i: (0, 0)),
            pl.BlockSpec((D_MODEL, D_MODEL), lambda i: (0, 0)),
            pl.BlockSpec((1, MEM_LEN, D_MODEL), lambda i: (i // per_seq, 0, 0)),
            pl.BlockSpec((1, MEM_LEN, D_MODEL), lambda i: (i // per_seq, 0, 1)),
            pl.BlockSpec((D_MODEL, D_MODEL), lambda i: (0, 0)),
        ],
        out_specs=pl.BlockSpec((tm, D_MODEL), lambda i: (i, 0)),
        out_shape=jax.ShapeDtypeStruct((m, D_MODEL), F32),
        compiler_params=_cparams("parallel"),
        name="cross_attention",
    )(x, g.reshape(1, D_MODEL), wq, kv, kv, wo)


def _ffn_kernel(x_ref, xh_ref, g_ref, w_ref, cw_ref, cb_ref, wd_ref, o_ref,
                h_ref, hh_ref, a_ref, act_ref, *, tm, tf, seq):
    i = pl.program_id(0)
    h_ref[...] = _rms(x_ref[...], g_ref[...]).astype(BF16)
    hh_ref[...] = _rms(xh_ref[...], g_ref[...]).astype(BF16)
    halo_ok = ((i * tm) % seq != 0).astype(F32)
    for n, c in enumerate(range(0, D_FF, tf)):
        a_buf = a_ref.at[n % 2]
        wa = w_ref[:, c:c + tf]
        a_buf[0:HALO, :] = jnp.dot(hh_ref[...], wa, preferred_element_type=F32) * halo_ok
        a_buf[HALO:, :] = jnp.dot(h_ref[...], wa, preferred_element_type=F32)
        u = jnp.dot(h_ref[...], w_ref[:, D_FF + c:D_FF + c + tf], preferred_element_type=F32)
        cw = cw_ref[:, c:c + tf]
        a = (cw[0:1] * a_buf[HALO - 2:HALO - 2 + tm, :] + cw[1:2] * a_buf[HALO - 1:HALO - 1 + tm, :]
             + cw[2:3] * a_buf[HALO:, :] + cb_ref[:, c:c + tf])
        act_ref[:, c:c + tf] = (jax.nn.gelu(a, approximate=True) * u).astype(BF16)
    o_ref[...] = x_ref[...] + jnp.dot(act_ref[...], wd_ref[...], preferred_element_type=F32)


def _conv_ffn(x, g, w_in, conv_w, conv_b, w_down, tm, tf, seq):
    m = x.shape[0]
    hb = tm // HALO
    return pl.pallas_call(
        functools.partial(_ffn_kernel, tm=tm, tf=tf, seq=seq),
        grid=(m // tm,),
        in_specs=[
            pl.BlockSpec((tm, D_MODEL), lambda i: (i, 0)),
            pl.BlockSpec((HALO, D_MODEL), lambda i: (jnp.maximum(i * hb - 1, 0), 0)),
            _resident((1, D_MODEL)), _resident((D_MODEL, 2 * D_FF)), _resident((3, D_FF)),
            _resident((1, D_FF)), _resident((D_FF, D_MODEL)),
        ],
        out_specs=pl.BlockSpec((tm, D_MODEL), lambda i: (i, 0)),
        out_shape=jax.ShapeDtypeStruct((m, D_MODEL), F32),
        scratch_shapes=[
            pltpu.VMEM((tm, D_MODEL), BF16),
            pltpu.VMEM((HALO, D_MODEL), BF16),
            pltpu.VMEM((2, tm + HALO, tf), F32),
            pltpu.VMEM((tm, D_FF), BF16),
        ],
        compiler_params=_cparams("parallel"),
        name="conv_ffn",
    )(x, x, g.reshape(1, D_MODEL), w_in, conv_w, conv_b.reshape(1, D_FF), w_down)


def kernel(x, mem, norm_mix, w_in, sc_conv_w, w_sb_out, w_sc_out, w_dsa_out, w_mix_o, norm_xa, norm_mem, w_xa_q, w_xa_kv, w_xa_o, norm_ffn, w_ffn_in, ffn_conv_w, ffn_conv_b, w_ffn_down, norm_final):
    batch, seq, _ = x.shape
    depth = w_in.shape[0]
    m = batch * seq
    xf = x.reshape(m, D_MODEL)
    memf = mem.reshape(batch * MEM_LEN, D_MODEL)
    o_sc, o_dsa, o_iq, o_ik, o_g = 3 * WIDTH, 6 * WIDTH, 9 * WIDTH, 10 * WIDTH, 10 * WIDTH + IDX_DIM + N_HEADS

    for l in range(depth):
        wl = w_in[l]
        qs = HEAD_DIM ** -0.5
        w_qkv = jnp.concatenate([
            wl[:, :WIDTH] * qs, wl[:, WIDTH:o_sc],
            wl[:, o_dsa:o_dsa + WIDTH] * (qs * LOG2E), wl[:, o_dsa + WIDTH:o_dsa + 2 * WIDTH],
            wl[:, o_iq:o_ik] * IDX_DIM ** -0.5], axis=1).astype(BF16)
        w_v = wl[:, o_dsa + 2 * WIDTH:o_iq].T.reshape(N_HEADS, HEAD_DIM, D_MODEL)
        w_vt = jnp.pad(w_v, ((0, 0), (0, BF16_ROWS), (0, 0))).reshape(N_HEADS * VT_ROWS, D_MODEL).astype(BF16)
        ones_rows = jnp.tile(jnp.arange(VT_ROWS) >= HEAD_DIM, N_HEADS).astype(F32).reshape(N_HEADS * VT_ROWS, 1)
        w_elt = jnp.concatenate([wl[:, o_g:], wl[:, o_sc:o_dsa]], axis=1).astype(BF16)
        w_ikw = jnp.pad(wl[:, o_ik:o_g], ((0, 0), (0, LANES - (o_g - o_ik)))).astype(BF16)

        qkv, zf, vt, ikw = _mixer_proj(xf, norm_mix[l], w_qkv, w_elt, w_vt, ones_rows, w_ikw, 256, 512)
        qkv = qkv.reshape(batch, seq, 6 * WIDTH)
        ikw = ikw.reshape(batch, seq, LANES)
        k_idx = ikw[:, :, :IDX_DIM].astype(BF16)

        ya = _sb_attention(qkv, 256).reshape(m, WIDTH)
        yc = _dsa_attention(qkv, vt, k_idx, ikw, 256, 128, 256).reshape(m, WIDTH)
        xf = _merge(xf, ya, yc, zf, sc_conv_w[l], w_sb_out[l].astype(BF16), w_sc_out[l].astype(BF16),
                    w_dsa_out[l].astype(BF16), w_mix_o[l].astype(BF16), 512, seq)

        kv = _norm_matmul(memf, norm_mem[l], w_xa_kv[l].astype(BF16), BF16, 512, 512)
        xf = _cross_attention(xf, norm_xa[l], w_xa_q[l].astype(BF16), kv.reshape(batch, MEM_LEN, 2 * D_MODEL),
                              w_xa_o[l].astype(BF16), 512, seq)

        xf = _conv_ffn(xf, norm_ffn[l], w_ffn_in[l].astype(BF16), ffn_conv_w[l], ffn_conv_b[l],
                       w_ffn_down[l].astype(BF16), 512, 256, seq)

    return _rmsnorm(xf, norm_final, 512).reshape(batch, seq, D_MODEL)
```

```python
import functools

import jax
import jax.numpy as jnp
from jax import lax
from jax.experimental import pallas as pl
from jax.experimental.pallas import tpu as pltpu

F32 = jnp.float32
BF16 = jnp.bfloat16
I32 = jnp.int32
I16 = jnp.int16
HALF_BIAS = 2 ** 15

D_MODEL = 1024
HEAD_DIM = 64
N_HEADS = 8
WIDTH = N_HEADS * HEAD_DIM
IDX_DIM = 64
TOPK_MAX = 256
MEM_LEN = 256
XA_HEADS = 4
XA_HEAD_DIM = D_MODEL // XA_HEADS
D_FF = 2816
RMS_EPS = 1e-6
LANES = 128
HALO = 8
BF16_ROWS = 16
VT_ROWS = HEAD_DIM + BF16_ROWS
LOG2E = 1.4426950408889634
INT_MIN = -2 ** 31
SB_DEAD = -104.0
VMEM_LIMIT = 56 * 1024 * 1024

NT_DIMS = (((1,), (1,)), ((), ()))


def _cparams(*sem):
    return pltpu.CompilerParams(dimension_semantics=sem, vmem_limit_bytes=VMEM_LIMIT)


def _rms(x, g):
    ms = jnp.mean(x * x, axis=-1, keepdims=True)
    return x * lax.rsqrt(ms + RMS_EPS) * g


def _norm_matmul_kernel(x_ref, g_ref, w_ref, o_ref, h_ref):
    @pl.when(pl.program_id(1) == 0)
    def _():
        h_ref[...] = _rms(x_ref[...], g_ref[...]).astype(BF16)

    o_ref[...] = jnp.dot(h_ref[...], w_ref[...], preferred_element_type=F32).astype(o_ref.dtype)


def _norm_matmul(x, g, w, out_dtype, tm, tn):
    m, k = x.shape
    n = w.shape[1]
    return pl.pallas_call(
        _norm_matmul_kernel,
        grid=(m // tm, n // tn),
        in_specs=[
            pl.BlockSpec((tm, k), lambda i, j: (i, 0)),
            pl.BlockSpec((1, k), lambda i, j: (0, 0)),
            pl.BlockSpec((k, tn), lambda i, j: (0, j)),
        ],
        out_specs=pl.BlockSpec((tm, tn), lambda i, j: (i, j)),
        out_shape=jax.ShapeDtypeStruct((m, n), out_dtype),
        scratch_shapes=[pltpu.VMEM((tm, k), BF16)],
        compiler_params=_cparams("parallel", "arbitrary"),
        name="norm_matmul",
    )(x, g.reshape(1, k), w)


def _resident(shape):
    return pl.BlockSpec(shape, lambda i: (0,) * len(shape), pipeline_mode=pl.Buffered(1))


def _mixer_proj_kernel(x_ref, g_ref, wq_ref, we_ref, wvt_ref, vb_ref, wik_ref,
                       qkv_ref, zf_ref, vt_ref, ikw_ref, h_ref, *, cn):
    h_ref[...] = _rms(x_ref[...], g_ref[...]).astype(BF16)
    for c in range(0, wq_ref.shape[1], cn):
        qkv_ref[:, c:c + cn] = jnp.dot(h_ref[...], wq_ref[:, c:c + cn],
                                       preferred_element_type=F32).astype(qkv_ref.dtype)
    for c in range(0, we_ref.shape[1], cn):
        zf_ref[:, c:c + cn] = jnp.dot(h_ref[...], we_ref[:, c:c + cn], preferred_element_type=F32)
    vt = lax.dot_general(wvt_ref[...], h_ref[...], NT_DIMS, preferred_element_type=F32)
    vt_ref[...] = (vt + vb_ref[...]).astype(vt_ref.dtype)
    ikw_ref[...] = jnp.dot(h_ref[...], wik_ref[...], preferred_element_type=F32)


def _mixer_proj(x, g, w_qkv, w_elt, w_vt, vt_bias, w_ikw, tm, cn):
    m, k = x.shape
    nq, ne, nv, ni = w_qkv.shape[1], w_elt.shape[1], w_vt.shape[0], w_ikw.shape[1]
    return pl.pallas_call(
        functools.partial(_mixer_proj_kernel, cn=cn),
        grid=(m // tm,),
        in_specs=[
            pl.BlockSpec((tm, k), lambda i: (i, 0)),
            _resident((1, k)), _resident((k, nq)), _resident((k, ne)), _resident((nv, k)),
            _resident((nv, 1)), _resident((k, ni)),
        ],
        out_specs=[
            pl.BlockSpec((tm, nq), lambda i: (i, 0)),
            pl.BlockSpec((tm, ne), lambda i: (i, 0)),
            pl.BlockSpec((nv, tm), lambda i: (0, i)),
            pl.BlockSpec((tm, ni), lambda i: (i, 0)),
        ],
        out_shape=[
            jax.ShapeDtypeStruct((m, nq), BF16),
            jax.ShapeDtypeStruct((m, ne), F32),
            jax.ShapeDtypeStruct((nv, m), BF16),
            jax.ShapeDtypeStruct((m, ni), F32),
        ],
        scratch_shapes=[pltpu.VMEM((tm, k), BF16)],
        compiler_params=_cparams("parallel"),
        name="mixer_proj",
    )(x, g.reshape(1, k), w_qkv, w_elt, w_vt, vt_bias, w_ikw)


def _rmsnorm_kernel(x_ref, g_ref, o_ref):
    o_ref[...] = _rms(x_ref[...], g_ref[...])


def _rmsnorm(x, g, tm):
    m, k = x.shape
    return pl.pallas_call(
        _rmsnorm_kernel,
        grid=(m // tm,),
        in_specs=[pl.BlockSpec((tm, k), lambda i: (i, 0)), pl.BlockSpec((1, k), lambda i: (0, 0))],
        out_specs=pl.BlockSpec((tm, k), lambda i: (i, 0)),
        out_shape=jax.ShapeDtypeStruct((m, k), F32),
        compiler_params=_cparams("parallel"),
        name="final_rmsnorm",
    )(x, g.reshape(1, k))


def _sb_kernel(q_ref, k_ref, v_ref, o_ref, acc_ref, c_ref, qx_ref, *, t):
    i = pl.program_id(1)
    acc_ref[...] = jnp.zeros_like(acc_ref)
    c_ref[...] = jnp.zeros_like(c_ref)
    rows = lax.broadcasted_iota(I32, (t, t), 0)
    cols = lax.broadcasted_iota(I32, (t, t), 1)
    later = (rows > cols).astype(BF16)
    sums = jnp.concatenate([later, jnp.ones((t, LANES), BF16)], axis=1)
    sums2 = jnp.concatenate([sums, sums], axis=0)
    diff = cols - rows
    qf = q_ref[0].astype(F32)
    upper = lax.broadcasted_iota(I32, (1, LANES), 1) >= HEAD_DIM
    for h in range(N_HEADS):
        pair = qf[:, (h // 2) * LANES:(h // 2 + 1) * LANES]
        keep = upper if h % 2 else jnp.logical_not(upper)
        qx_ref[h] = jnp.where(keep, pair, 0.0).astype(BF16)

    def cond(state):
        j, cmax = state
        return jnp.logical_and(j >= 0, cmax > SB_DEAD)

    def body(state):
        j, _ = state
        ks = pl.multiple_of(j * t, t)
        causal = diff < (i - j) * t
        cm = jnp.full((t, LANES), -jnp.inf, F32)
        for h in range(N_HEADS):
            lo, hi = (h // 2) * LANES, (h // 2 + 1) * LANES
            z = lax.dot_general(qx_ref[h], k_ref[0, pl.ds(ks, t), lo:hi], NT_DIMS, preferred_element_type=F32)
            sp = jnp.maximum(z, 0.0) + jnp.log(1.0 + jnp.exp(-jnp.abs(z)))
            lg1m = jnp.where(causal, -sp, 0.0)
            l_hi = lg1m.astype(BF16)
            l_lo = (lg1m - l_hi.astype(F32)).astype(BF16)
            s = jnp.dot(jnp.concatenate([l_hi, l_lo], axis=1), sums2, preferred_element_type=F32)
            c_h = c_ref[h]
            c_tile = jnp.concatenate([c_h] * (t // LANES), axis=1)
            w = jnp.where(causal, jnp.exp(z - sp + c_tile + s[:, :t]), 0.0)
            acc_ref[h] += jnp.dot(w.astype(BF16), v_ref[0, pl.ds(ks, t), lo:hi], preferred_element_type=F32)
            c_new = c_h + s[:, t:]
            c_ref[h] = c_new
            cm = jnp.maximum(cm, c_new)
        return j - 1, jnp.max(cm)

    lax.while_loop(cond, body, (i, jnp.asarray(0.0, F32)))
    for p in range(N_HEADS // 2):
        both = jnp.where(upper, acc_ref[2 * p + 1], acc_ref[2 * p])
        o_ref[0, :, p * LANES:(p + 1) * LANES] = both.astype(o_ref.dtype)


def _sb_attention(qkv, t):
    b, s, _ = qkv.shape
    return pl.pallas_call(
        functools.partial(_sb_kernel, t=t),
        grid=(b, s // t),
        in_specs=[
            pl.BlockSpec((1, t, WIDTH), lambda bi, i: (bi, i, 0)),
            pl.BlockSpec((1, s, WIDTH), lambda bi, i: (bi, 0, 1)),
            pl.BlockSpec((1, s, WIDTH), lambda bi, i: (bi, 0, 2)),
        ],
        out_specs=pl.BlockSpec((1, t, WIDTH), lambda bi, i: (bi, i, 0)),
        out_shape=jax.ShapeDtypeStruct((b, s, WIDTH), BF16),
        scratch_shapes=[
            pltpu.VMEM((N_HEADS, t, LANES), F32),
            pltpu.VMEM((N_HEADS, t, LANES), F32),
            pltpu.VMEM((N_HEADS, t, LANES), BF16),
        ],
        compiler_params=_cparams("parallel", "arbitrary"),
        name="sb_attention",
    )(qkv, qkv, qkv)


def _dsa_kernel(q_ref, k_ref, vt_ref, qi_ref, ki_ref, wi_ref, o_ref,
                key_ref, half_ref, bias2_ref, lga_ref, lgb_ref, p2_ref, qx_ref, qih_ref, acc_ref, m_ref, alpha_ref, tie_ref,
                *, tq, tc, ta, top_k):
    i = pl.program_id(1)
    nchunks = (i + 1) * (tq // tc)
    q_pos = i * tq + lax.broadcasted_iota(I32, (1, tq), 1)
    row = lax.broadcasted_iota(I32, (tc, 1), 0)

    qi = qi_ref[0]
    qf = q_ref[0].astype(F32)
    upper = lax.broadcasted_iota(I32, (1, LANES), 1) >= HEAD_DIM
    for h in range(N_HEADS):
        qih_ref[h] = qi[:, h * IDX_DIM:(h + 1) * IDX_DIM]
        pair = qf[:, (h // 2) * LANES:(h // 2 + 1) * LANES]
        keep = upper if h % 2 else jnp.logical_not(upper)
        qx_ref[h] = jnp.where(keep, pair, 0.0).astype(BF16)
    wt = jnp.transpose(wi_ref[0])[IDX_DIM:IDX_DIM + N_HEADS, :] * (N_HEADS ** -0.5)

    def score_chunk(c, carry):
        ks = pl.multiple_of(c * tc, tc)
        kc = ki_ref[0, pl.ds(ks, tc), :]
        sc = jnp.zeros((tc, tq), F32)
        for h in range(N_HEADS):
            lg = lax.dot_general(kc, qih_ref[h], NT_DIMS, preferred_element_type=F32)
            sc = sc + wt[h:h + 1, :] * jnp.maximum(lg, 0.0)
        sc = jnp.where(sc == 0.0, 0.0, sc)
        bits = pltpu.bitcast(sc, I32)
        skey = bits ^ ((bits >> 31) & 0x7FFFFFFF)
        key = jnp.where(ks + row <= q_pos, skey, INT_MIN)
        key_ref[pl.ds(ks, tc), :] = key
        half_ref[pl.ds(ks, tc), :] = (key >> 16).astype(I16)
        return carry

    def score_pair(c, carry):
        return score_chunk(2 * c + 1, score_chunk(2 * c, carry))

    lax.fori_loop(0, nchunks // 2, score_pair, 0)

    slab = 32

    def count(pred):
        def count_chunk(c, cnt):
            ks = pl.multiple_of(c * tc, tc)
            for u in range(tc // slab):
                cnt = cnt + jnp.where(pred(key_ref[pl.ds(ks + u * slab, slab), :]), 1, 0)
            return cnt

        cnt = lax.fori_loop(0, nchunks, count_chunk, jnp.zeros((slab, tq), I32))
        return jnp.sum(cnt.astype(F32), axis=0, keepdims=True)

    def count_half(cand):
        cand_b = jnp.broadcast_to(cand.astype(I16), (slab, tq))

        def count_chunk(c, cnt):
            ks = pl.multiple_of(c * tc, tc)
            for u in range(tc // slab):
                hit = half_ref[pl.ds(ks + u * slab, slab), :] >= cand_b
                cnt = cnt + jnp.where(hit, jnp.ones((), I16), jnp.zeros((), I16))
            return cnt

        cnt = lax.fori_loop(0, nchunks, count_chunk, jnp.zeros((slab, tq), I16))
        return jnp.sum(cnt.astype(F32), axis=0, keepdims=True)

    def half_search(n_above, n_ge0):
        def bit_step(b, state):
            t_u, n_ge = state
            cand = t_u | lax.shift_left(jnp.asarray(1, I32), 15 - b)
            cnt = n_above + count_half(cand - HALF_BIAS)
            ok = cnt >= top_k
            return jnp.where(ok, cand, t_u), jnp.where(ok, cnt, n_ge)

        return lax.fori_loop(0, 16, bit_step, (jnp.zeros((1, tq), I32), n_ge0))

    zero = jnp.zeros((1, tq), F32)
    hi_u, n_ge = half_search(zero, zero)
    hi_s = hi_u - HALF_BIAS
    n_above = jnp.where(hi_s == HALF_BIAS - 1, zero, count_half(jnp.minimum(hi_s + 1, HALF_BIAS - 1)))

    def lower_chunk(c, carry):
        ks = pl.multiple_of(c * tc, tc)
        kk = key_ref[pl.ds(ks, tc), :]
        low = jnp.where((kk >> 16) == hi_s, (kk & 0xFFFF) - HALF_BIAS, -HALF_BIAS)
        half_ref[pl.ds(ks, tc), :] = low.astype(I16)
        return carry

    lax.fori_loop(0, nchunks, lower_chunk, 0)
    lo_u, n_ge = half_search(n_above, n_ge)
    thr = jnp.maximum(hi_s * (2 * HALF_BIAS) + lo_u, INT_MIN + 1)
    has_ties = jnp.max(n_ge) > top_k

    def tie_quota():
        thr_b = jnp.broadcast_to(thr, (slab, tq))
        return top_k - count(lambda kk: kk > thr_b)

    quota = lax.cond(has_ties, tie_quota, lambda: jnp.zeros((1, tq), F32))

    m_ref[...] = jnp.full_like(m_ref, -jnp.inf)
    acc_ref[...] = jnp.zeros_like(acc_ref)
    tie_ref[...] = jnp.zeros_like(tie_ref)

    def select_chunk(c, slot):
        ks = pl.multiple_of(c * ta, ta)
        kk = key_ref[pl.ds(ks, ta), :]

        def plain_bias():
            return jnp.where(kk >= thr, 0.0, -jnp.inf)

        def tie_bias():
            tie = kk == thr
            before = (lax.broadcasted_iota(I32, (ta, ta), 1) <= lax.broadcasted_iota(I32, (ta, ta), 0))
            seen = tie_ref[...] + jnp.dot(before.astype(BF16), jnp.where(tie, 1.0, 0.0).astype(BF16),
                                          preferred_element_type=F32)
            tie_ref[...] = seen[ta - 1:ta, :]
            take = jnp.logical_or(kk > thr, jnp.logical_and(tie, seen <= quota))
            return jnp.where(take, 0.0, -jnp.inf)

        bias2_ref[slot] = lax.cond(has_ties, tie_bias, plain_bias)

    def attend_chunk(c, slot):
        bias_ref, lg_ref, p_ref = bias2_ref.at[slot], (lga_ref, lgb_ref)[slot], p2_ref.at[slot]
        ks = pl.multiple_of(c * ta, ta)
        for h in range(N_HEADS):
            kp = k_ref[0, pl.ds(ks, ta), (h // 2) * LANES:(h // 2 + 1) * LANES]
            lg_ref[h] = lax.dot_general(kp, qx_ref[h], NT_DIMS, preferred_element_type=F32)
        part_max = [jnp.full((HALO, tq), -jnp.inf, F32) for _ in range(N_HEADS)]
        for r in range(0, ta, HALO):
            bias = bias_ref[r:r + HALO, :]
            for h in range(N_HEADS):
                lg = lg_ref[h, r:r + HALO, :] + bias
                lg_ref[h, r:r + HALO, :] = lg
                part_max[h] = jnp.maximum(part_max[h], lg)
        for h in range(N_HEADS):
            m_old = m_ref[h:h + 1, :]
            m_new = jnp.maximum(m_old, jnp.max(part_max[h], axis=0, keepdims=True))
            m_fin = jnp.where(m_new == -jnp.inf, 0.0, m_new)
            m_rows = jnp.broadcast_to(m_fin, (BF16_ROWS, tq))
            for r in range(0, ta, BF16_ROWS):
                p_ref[h, r:r + BF16_ROWS, :] = jnp.exp2(lg_ref[h, r:r + BF16_ROWS, :] - m_rows).astype(BF16)
            alpha_ref[h:h + 1, :] = jnp.exp2(m_old - m_fin)
            m_ref[h:h + 1, :] = m_new
        for h in range(N_HEADS):
            lo, hi = h * VT_ROWS, (h + 1) * VT_ROWS
            pv = jnp.dot(vt_ref[lo:hi, pl.ds(ks, ta)], p_ref[h], preferred_element_type=F32)
            acc_ref[lo:hi, :] = alpha_ref[h:h + 1, :] * acc_ref[lo:hi, :] + pv

    def attend_pair(c, carry):
        select_chunk(2 * c, 0)
        select_chunk(2 * c + 1, 1)
        attend_chunk(2 * c, 0)
        attend_chunk(2 * c + 1, 1)
        return carry

    n_att = (i + 1) * (tq // ta)
    lax.fori_loop(0, n_att // 2, attend_pair, 0)

    @pl.when(n_att % 2 == 1)
    def _():
        select_chunk(n_att - 1, 0)
        attend_chunk(n_att - 1, 0)

    outs = []
    for h in range(N_HEADS):
        lo = h * VT_ROWS
        outs.append(acc_ref[lo:lo + HEAD_DIM, :] / acc_ref[lo + HEAD_DIM:lo + HEAD_DIM + 1, :])
    o_ref[0] = jnp.transpose(jnp.concatenate(outs, axis=0)).astype(o_ref.dtype)


def _dsa_attention(qkv, vt, k_idx, idx_kw, tq, tc, ta):
    b, s, _ = qkv.shape
    top_k = min(TOPK_MAX, s // 4)
    return pl.pallas_call(
        functools.partial(_dsa_kernel, tq=tq, tc=tc, ta=ta, top_k=top_k),
        grid=(b, s // tq),
        in_specs=[
            pl.BlockSpec((1, tq, WIDTH), lambda bi, i: (bi, i, 3)),
            pl.BlockSpec((1, s, WIDTH), lambda bi, i: (bi, 0, 4), pipeline_mode=pl.Buffered(1)),
            pl.BlockSpec((N_HEADS * VT_ROWS, s), lambda bi, i: (0, bi), pipeline_mode=pl.Buffered(1)),
            pl.BlockSpec((1, tq, WIDTH), lambda bi, i: (bi, i, 5)),
            pl.BlockSpec((1, s, IDX_DIM), lambda bi, i: (bi, 0, 0), pipeline_mode=pl.Buffered(1)),
            pl.BlockSpec((1, tq, LANES), lambda bi, i: (bi, i, 0)),
        ],
        out_specs=pl.BlockSpec((1, tq, WIDTH), lambda bi, i: (bi, i, 0)),
        out_shape=jax.ShapeDtypeStruct((b, s, WIDTH), BF16),
        scratch_shapes=[
            pltpu.VMEM((s, tq), I32),
            pltpu.VMEM((s, tq), I16),
            pltpu.VMEM((2, ta, tq), F32),
            pltpu.VMEM((N_HEADS, ta, tq), F32),
            pltpu.VMEM((N_HEADS, ta, tq), F32),
            pltpu.VMEM((2, N_HEADS, ta, tq), BF16),
            pltpu.VMEM((N_HEADS, tq, LANES), BF16),
            pltpu.VMEM((N_HEADS, tq, IDX_DIM), BF16),
            pltpu.VMEM((N_HEADS * VT_ROWS, tq), F32),
            pltpu.VMEM((N_HEADS, tq), F32),
            pltpu.VMEM((N_HEADS, tq), F32),
            pltpu.VMEM((1, tq), F32),
        ],
        compiler_params=_cparams("parallel", "arbitrary"),
        name="dsa_attention",
    )(qkv, qkv, vt, qkv, k_idx, idx_kw)


def _merge_kernel(x_ref, ya_ref, yc_ref, b_ref, c_ref, xi_ref, ch_ref, xh_ref,
                  g0_ref, g1_ref, g2_ref, cw_ref, wa_ref, wb_ref, wc_ref, wo_ref,
                  o_ref, u_ref, *, tm, seq):
    i = pl.program_id(0)
    halo_ok = ((i * tm) % seq != 0).astype(F32)
    u_ref[0:HALO, :] = ch_ref[...] * xh_ref[...] * halo_ok
    u_ref[HALO:, :] = c_ref[...] * xi_ref[...]
    cw = cw_ref[...]
    conv = (cw[0:1] * u_ref[HALO - 2:HALO - 2 + tm, :] + cw[1:2] * u_ref[HALO - 1:HALO - 1 + tm, :]
            + cw[2:3] * u_ref[HALO:, :])
    yb = (b_ref[...] * conv).astype(BF16)
    pa = jnp.dot(ya_ref[...], wa_ref[...], preferred_element_type=F32)
    pb = jnp.dot(yb, wb_ref[...], preferred_element_type=F32)
    pc = jnp.dot(yc_ref[...], wc_ref[...], preferred_element_type=F32)
    merged = (jax.nn.sigmoid(g0_ref[...]) * pa + jax.nn.sigmoid(g1_ref[...]) * pb
              + jax.nn.sigmoid(g2_ref[...]) * pc)
    o_ref[...] = x_ref[...] + jnp.dot(merged.astype(BF16), wo_ref[...], preferred_element_type=F32)


def _merge(x, ya, yc, zf, conv_w, wa, wb, wc, wo, tm, seq):
    m = x.shape[0]
    hb = tm // HALO
    row = lambda c: (lambda i: (i, c))
    halo = lambda c: (lambda i: (jnp.maximum(i * hb - 1, 0), c))
    const = lambda i: (0, 0)
    return pl.pallas_call(
        functools.partial(_merge_kernel, tm=tm, seq=seq),
        grid=(m // tm,),
        in_specs=[
            pl.BlockSpec((tm, D_MODEL), row(0)),
            pl.BlockSpec((tm, WIDTH), row(0)),
            pl.BlockSpec((tm, WIDTH), row(0)),
            pl.BlockSpec((tm, WIDTH), row(6)),
            pl.BlockSpec((tm, WIDTH), row(7)),
            pl.BlockSpec((tm, WIDTH), row(8)),
            pl.BlockSpec((HALO, WIDTH), halo(7)),
            pl.BlockSpec((HALO, WIDTH), halo(8)),
            pl.BlockSpec((tm, D_MODEL), row(0)),
            pl.BlockSpec((tm, D_MODEL), row(1)),
            pl.BlockSpec((tm, D_MODEL), row(2)),
            pl.BlockSpec((3, WIDTH), const),
            pl.BlockSpec((WIDTH, D_MODEL), const),
            pl.BlockSpec((WIDTH, D_MODEL), const),
            pl.BlockSpec((WIDTH, D_MODEL), const),
            pl.BlockSpec((D_MODEL, D_MODEL), const),
        ],
        out_specs=pl.BlockSpec((tm, D_MODEL), row(0)),
        out_shape=jax.ShapeDtypeStruct((m, D_MODEL), F32),
        scratch_shapes=[pltpu.VMEM((tm + HALO, WIDTH), F32)],
        compiler_params=_cparams("parallel"),
        name="merge",
    )(x, ya, yc, zf, zf, zf, zf, zf, zf, zf, zf, conv_w, wa, wb, wc, wo)


def _xa_kernel(x_ref, g_ref, wq_ref, k_ref, v_ref, wo_ref, o_ref):
    x = x_ref[...]
    h = _rms(x, g_ref[...]).astype(BF16)
    q = jnp.dot(h, wq_ref[...], preferred_element_type=F32) * (XA_HEAD_DIM ** -0.5)
    q = q.astype(BF16)
    outs = []
    for hh in range(XA_HEADS):
        lo, hi = hh * XA_HEAD_DIM, (hh + 1) * XA_HEAD_DIM
        lg = lax.dot_general(q[:, lo:hi], k_ref[0, :, lo:hi], NT_DIMS, preferred_element_type=F32)
        p = jnp.exp(lg - jnp.max(lg, axis=1, keepdims=True))
        p = p / jnp.sum(p, axis=1, keepdims=True)
        outs.append(jnp.dot(p.astype(BF16), v_ref[0, :, lo:hi], preferred_element_type=F32))
    o = jnp.concatenate(outs, axis=1).astype(BF16)
    o_ref[...] = x + jnp.dot(o, wo_ref[...], preferred_element_type=F32)


def _cross_attention(x, g, wq, kv, wo, tm, seq):
    m = x.shape[0]
    per_seq = seq // tm
    return pl.pallas_call(
        _xa_kernel,
        grid=(m // tm,),
        in_specs=[
            pl.BlockSpec((tm, D_MODEL), lambda i: (i, 0)),
            pl.BlockSpec((1, D_MODEL), lambda i: (0, 0)),
            pl.BlockSpec((D_MODEL, D_MODEL), lambda i: (0, 0)),
            pl.BlockSpec((1, MEM_LEN, D_MODEL), lambda i: (i // per_seq, 0, 0)),
            pl.BlockSpec((1, MEM_LEN, D_MODEL), lambda i: (i // per_seq, 0, 1)),
            pl.BlockSpec((D_MODEL, D_MODEL), lambda i: (0, 0)),
        ],
        out_specs=pl.BlockSpec((tm, D_MODEL), lambda i: (i, 0)),
        out_shape=jax.ShapeDtypeStruct((m, D_MODEL), F32),
        compiler_params=_cparams("parallel"),
        name="cross_attention",
    )(x, g.reshape(1, D_MODEL), wq, kv, kv, wo)


def _ffn_kernel(x_ref, xh_ref, g_ref, w_ref, cw_ref, cb_ref, wd_ref, o_ref,
                h_ref, hh_ref, a_ref, act_ref, *, tm, tf, seq):
    i = pl.program_id(0)
    h_ref[...] = _rms(x_ref[...], g_ref[...]).astype(BF16)
    hh_ref[...] = _rms(xh_ref[...], g_ref[...]).astype(BF16)
    halo_ok = ((i * tm) % seq != 0).astype(F32)
    for n, c in enumerate(range(0, D_FF, tf)):
        a_buf = a_ref.at[n % 2]
        wa = w_ref[:, c:c + tf]
        a_buf[0:HALO, :] = jnp.dot(hh_ref[...], wa, preferred_element_type=F32) * halo_ok
        a_buf[HALO:, :] = jnp.dot(h_ref[...], wa, preferred_element_type=F32)
        u = jnp.dot(h_ref[...], w_ref[:, D_FF + c:D_FF + c + tf], preferred_element_type=F32)
        cw = cw_ref[:, c:c + tf]
        a = (cw[0:1] * a_buf[HALO - 2:HALO - 2 + tm, :] + cw[1:2] * a_buf[HALO - 1:HALO - 1 + tm, :]
             + cw[2:3] * a_buf[HALO:, :] + cb_ref[:, c:c + tf])
        act_ref[:, c:c + tf] = (jax.nn.gelu(a, approximate=True) * u).astype(BF16)
    o_ref[...] = x_ref[...] + jnp.dot(act_ref[...], wd_ref[...], preferred_element_type=F32)


def _conv_ffn(x, g, w_in, conv_w, conv_b, w_down, tm, tf, seq):
    m = x.shape[0]
    hb = tm // HALO
    return pl.pallas_call(
        functools.partial(_ffn_kernel, tm=tm, tf=tf, seq=seq),
        grid=(m // tm,),
        in_specs=[
            pl.BlockSpec((tm, D_MODEL), lambda i: (i, 0)),
            pl.BlockSpec((HALO, D_MODEL), lambda i: (jnp.maximum(i * hb - 1, 0), 0)),
            _resident((1, D_MODEL)), _resident((D_MODEL, 2 * D_FF)), _resident((3, D_FF)),
            _resident((1, D_FF)), _resident((D_FF, D_MODEL)),
        ],
        out_specs=pl.BlockSpec((tm, D_MODEL), lambda i: (i, 0)),
        out_shape=jax.ShapeDtypeStruct((m, D_MODEL), F32),
        scratch_shapes=[
            pltpu.VMEM((tm, D_MODEL), BF16),
            pltpu.VMEM((HALO, D_MODEL), BF16),
            pltpu.VMEM((2, tm + HALO, tf), F32),
            pltpu.VMEM((tm, D_FF), BF16),
        ],
        compiler_params=_cparams("parallel"),
        name="conv_ffn",
    )(x, x, g.reshape(1, D_MODEL), w_in, conv_w, conv_b.reshape(1, D_FF), w_down)


def kernel(x, mem, norm_mix, w_in, sc_conv_w, w_sb_out, w_sc_out, w_dsa_out, w_mix_o, norm_xa, norm_mem, w_xa_q, w_xa_kv, w_xa_o, norm_ffn, w_ffn_in, ffn_conv_w, ffn_conv_b, w_ffn_down, norm_final):
    batch, seq, _ = x.shape
    depth = w_in.shape[0]
    m = batch * seq
    xf = x.reshape(m, D_MODEL)
    memf = mem.reshape(batch * MEM_LEN, D_MODEL)
    o_sc, o_dsa, o_iq, o_ik, o_g = 3 * WIDTH, 6 * WIDTH, 9 * WIDTH, 10 * WIDTH, 10 * WIDTH + IDX_DIM + N_HEADS

    for l in range(depth):
        wl = w_in[l]
        qs = HEAD_DIM ** -0.5
        w_qkv = jnp.concatenate([
            wl[:, :WIDTH] * qs, wl[:, WIDTH:o_sc],
            wl[:, o_dsa:o_dsa + WIDTH] * (qs * LOG2E), wl[:, o_dsa + WIDTH:o_dsa + 2 * WIDTH],
            wl[:, o_iq:o_ik] * IDX_DIM ** -0.5], axis=1).astype(BF16)
        w_v = wl[:, o_dsa + 2 * WIDTH:o_iq].T.reshape(N_HEADS, HEAD_DIM, D_MODEL)
        w_vt = jnp.pad(w_v, ((0, 0), (0, BF16_ROWS), (0, 0))).reshape(N_HEADS * VT_ROWS, D_MODEL).astype(BF16)
        ones_rows = jnp.tile(jnp.arange(VT_ROWS) >= HEAD_DIM, N_HEADS).astype(F32).reshape(N_HEADS * VT_ROWS, 1)
        w_elt = jnp.concatenate([wl[:, o_g:], wl[:, o_sc:o_dsa]], axis=1).astype(BF16)
        w_ikw = jnp.pad(wl[:, o_ik:o_g], ((0, 0), (0, LANES - (o_g - o_ik)))).astype(BF16)

        qkv, zf, vt, ikw = _mixer_proj(xf, norm_mix[l], w_qkv, w_elt, w_vt, ones_rows, w_ikw, 256, 512)
        qkv = qkv.reshape(batch, seq, 6 * WIDTH)
        ikw = ikw.reshape(batch, seq, LANES)
        k_idx = ikw[:, :, :IDX_DIM].astype(BF16)

        ya = _sb_attention(qkv, 256).reshape(m, WIDTH)
        yc = _dsa_attention(qkv, vt, k_idx, ikw, 256, 128, 256).reshape(m, WIDTH)
        xf = _merge(xf, ya, yc, zf, sc_conv_w[l], w_sb_out[l].astype(BF16), w_sc_out[l].astype(BF16),
                    w_dsa_out[l].astype(BF16), w_mix_o[l].astype(BF16), 512, seq)

        kv = _norm_matmul(memf, norm_mem[l], w_xa_kv[l].astype(BF16), BF16, 512, 512)
        xf = _cross_attention(xf, norm_xa[l], w_xa_q[l].astype(BF16), kv.reshape(batch, MEM_LEN, 2 * D_MODEL),
                              w_xa_o[l].astype(BF16), 512, seq)

        xf = _conv_ffn(xf, norm_ffn[l], w_ffn_in[l].astype(BF16), ffn_conv_w[l], ffn_conv_b[l],
                       w_ffn_down[l].astype(BF16), 512, 256, seq)

    return _rmsnorm(xf, norm_final, 512).reshape(batch, seq, D_MODEL)
```

```python
import functools

import jax
import jax.numpy as jnp
from jax import lax
from jax.experimental import pallas as pl
from jax.experimental.pallas import tpu as pltpu

F32 = jnp.float32
BF16 = jnp.bfloat16
I32 = jnp.int32
I16 = jnp.int16
HALF_BIAS = 2 ** 15

D_MODEL = 1024
HEAD_DIM = 64
N_HEADS = 8
WIDTH = N_HEADS * HEAD_DIM
IDX_DIM = 64
TOPK_MAX = 256
MEM_LEN = 256
XA_HEADS = 4
XA_HEAD_DIM = D_MODEL // XA_HEADS
D_FF = 2816
RMS_EPS = 1e-6
LANES = 128
HALO = 8
BF16_ROWS = 16
VT_ROWS = HEAD_DIM + BF16_ROWS
LOG2E = 1.4426950408889634
INT_MIN = -2 ** 31
SB_DEAD = -151.0
VMEM_LIMIT = 56 * 1024 * 1024

NT_DIMS = (((1,), (1,)), ((), ()))


def _cparams(*sem):
    return pltpu.CompilerParams(dimension_semantics=sem, vmem_limit_bytes=VMEM_LIMIT)


def _rms(x, g):
    ms = jnp.mean(x * x, axis=-1, keepdims=True)
    return x * lax.rsqrt(ms + RMS_EPS) * g


def _norm_matmul_kernel(x_ref, g_ref, w_ref, o_ref, h_ref):
    @pl.when(pl.program_id(1) == 0)
    def _():
        h_ref[...] = _rms(x_ref[...], g_ref[...]).astype(BF16)

    o_ref[...] = jnp.dot(h_ref[...], w_ref[...], preferred_element_type=F32).astype(o_ref.dtype)


def _norm_matmul(x, g, w, out_dtype, tm, tn):
    m, k = x.shape
    n = w.shape[1]
    return pl.pallas_call(
        _norm_matmul_kernel,
        grid=(m // tm, n // tn),
        in_specs=[
            pl.BlockSpec((tm, k), lambda i, j: (i, 0)),
            pl.BlockSpec((1, k), lambda i, j: (0, 0)),
            pl.BlockSpec((k, tn), lambda i, j: (0, j)),
        ],
        out_specs=pl.BlockSpec((tm, tn), lambda i, j: (i, j)),
        out_shape=jax.ShapeDtypeStruct((m, n), out_dtype),
        scratch_shapes=[pltpu.VMEM((tm, k), BF16)],
        compiler_params=_cparams("parallel", "arbitrary"),
        name="norm_matmul",
    )(x, g.reshape(1, k), w)


def _resident(shape):
    return pl.BlockSpec(shape, lambda i: (0,) * len(shape), pipeline_mode=pl.Buffered(1))


def _mixer_proj_kernel(x_ref, g_ref, wq_ref, we_ref, wvt_ref, vb_ref, wik_ref,
                       qkv_ref, zf_ref, vt_ref, ikw_ref, h_ref, *, cn):
    h_ref[...] = _rms(x_ref[...], g_ref[...]).astype(BF16)
    for c in range(0, wq_ref.shape[1], cn):
        qkv_ref[:, c:c + cn] = jnp.dot(h_ref[...], wq_ref[:, c:c + cn],
                                       preferred_element_type=F32).astype(qkv_ref.dtype)
    for c in range(0, we_ref.shape[1], cn):
        zf_ref[:, c:c + cn] = jnp.dot(h_ref[...], we_ref[:, c:c + cn], preferred_element_type=F32)
    vt = lax.dot_general(wvt_ref[...], h_ref[...], NT_DIMS, preferred_element_type=F32)
    vt_ref[...] = (vt + vb_ref[...]).astype(vt_ref.dtype)
    ikw_ref[...] = jnp.dot(h_ref[...], wik_ref[...], preferred_element_type=F32)


def _mixer_proj(x, g, w_qkv, w_elt, w_vt, vt_bias, w_ikw, tm, cn):
    m, k = x.shape
    nq, ne, nv, ni = w_qkv.shape[1], w_elt.shape[1], w_vt.shape[0], w_ikw.shape[1]
    return pl.pallas_call(
        functools.partial(_mixer_proj_kernel, cn=cn),
        grid=(m // tm,),
        in_specs=[
            pl.BlockSpec((tm, k), lambda i: (i, 0)),
            _resident((1, k)), _resident((k, nq)), _resident((k, ne)), _resident((nv, k)),
            _resident((nv, 1)), _resident((k, ni)),
        ],
        out_specs=[
            pl.BlockSpec((tm, nq), lambda i: (i, 0)),
            pl.BlockSpec((tm, ne), lambda i: (i, 0)),
            pl.BlockSpec((nv, tm), lambda i: (0, i)),
            pl.BlockSpec((tm, ni), lambda i: (i, 0)),
        ],
        out_shape=[
            jax.ShapeDtypeStruct((m, nq), BF16),
            jax.ShapeDtypeStruct((m, ne), F32),
            jax.ShapeDtypeStruct((nv, m), BF16),
            jax.ShapeDtypeStruct((m, ni), F32),
        ],
        scratch_shapes=[pltpu.VMEM((tm, k), BF16)],
        compiler_params=_cparams("parallel"),
        name="mixer_proj",
    )(x, g.reshape(1, k), w_qkv, w_elt, w_vt, vt_bias, w_ikw)


def _rmsnorm_kernel(x_ref, g_ref, o_ref):
    o_ref[...] = _rms(x_ref[...], g_ref[...])


def _rmsnorm(x, g, tm):
    m, k = x.shape
    return pl.pallas_call(
        _rmsnorm_kernel,
        grid=(m // tm,),
        in_specs=[pl.BlockSpec((tm, k), lambda i: (i, 0)), pl.BlockSpec((1, k), lambda i: (0, 0))],
        out_specs=pl.BlockSpec((tm, k), lambda i: (i, 0)),
        out_shape=jax.ShapeDtypeStruct((m, k), F32),
        compiler_params=_cparams("parallel"),
        name="final_rmsnorm",
    )(x, g.reshape(1, k))


def _sb_kernel(q_ref, k_ref, v_ref, o_ref, acc_ref, c_ref, qx_ref, *, t):
    i = pl.program_id(1)
    acc_ref[...] = jnp.zeros_like(acc_ref)
    c_ref[...] = jnp.zeros_like(c_ref)
    rows = lax.broadcasted_iota(I32, (t, t), 0)
    cols = lax.broadcasted_iota(I32, (t, t), 1)
    later = (rows > cols).astype(BF16)
    sums = jnp.concatenate([later, jnp.ones((t, LANES), BF16)], axis=1)
    sums2 = jnp.concatenate([sums, sums], axis=0)
    diff = cols - rows
    qf = q_ref[0].astype(F32)
    upper = lax.broadcasted_iota(I32, (1, LANES), 1) >= HEAD_DIM
    for h in range(N_HEADS):
        pair = qf[:, (h // 2) * LANES:(h // 2 + 1) * LANES]
        keep = upper if h % 2 else jnp.logical_not(upper)
        qx_ref[h] = jnp.where(keep, pair, 0.0).astype(BF16)

    def cond(state):
        j, cmax = state
        return jnp.logical_and(j >= 0, cmax > SB_DEAD)

    def body(state):
        j, _ = state
        ks = pl.multiple_of(j * t, t)
        causal = diff < (i - j) * t
        cm = jnp.full((t, LANES), -jnp.inf, F32)
        for h in range(N_HEADS):
            lo, hi = (h // 2) * LANES, (h // 2 + 1) * LANES
            z = lax.dot_general(qx_ref[h], k_ref[0, pl.ds(ks, t), lo:hi], NT_DIMS, preferred_element_type=F32)
            sp = jnp.maximum(z, 0.0) + jnp.log2(1.0 + jnp.exp2(-jnp.abs(z)))
            lg1m = jnp.where(causal, -sp, 0.0)
            l_hi = lg1m.astype(BF16)
            l_lo = (lg1m - l_hi.astype(F32)).astype(BF16)
            s = jnp.dot(jnp.concatenate([l_hi, l_lo], axis=1), sums2, preferred_element_type=F32)
            c_h = c_ref[h]
            c_tile = jnp.concatenate([c_h] * (t // LANES), axis=1)
            w = jnp.where(causal, jnp.exp2(z - sp + c_tile + s[:, :t]), 0.0)
            acc_ref[h] += jnp.dot(w.astype(BF16), v_ref[0, pl.ds(ks, t), lo:hi], preferred_element_type=F32)
            c_new = c_h + s[:, t:]
            c_ref[h] = c_new
            cm = jnp.maximum(cm, c_new)
        return j - 1, jnp.max(cm)

    lax.while_loop(cond, body, (i, jnp.asarray(0.0, F32)))
    for p in range(N_HEADS // 2):
        both = jnp.where(upper, acc_ref[2 * p + 1], acc_ref[2 * p])
        o_ref[0, :, p * LANES:(p + 1) * LANES] = both.astype(o_ref.dtype)


def _sb_attention(qkv, t):
    b, s, _ = qkv.shape
    return pl.pallas_call(
        functools.partial(_sb_kernel, t=t),
        grid=(b, s // t),
        in_specs=[
            pl.BlockSpec((1, t, WIDTH), lambda bi, i: (bi, i, 0)),
            pl.BlockSpec((1, s, WIDTH), lambda bi, i: (bi, 0, 1)),
            pl.BlockSpec((1, s, WIDTH), lambda bi, i: (bi, 0, 2)),
        ],
        out_specs=pl.BlockSpec((1, t, WIDTH), lambda bi, i: (bi, i, 0)),
        out_shape=jax.ShapeDtypeStruct((b, s, WIDTH), BF16),
        scratch_shapes=[
            pltpu.VMEM((N_HEADS, t, LANES), F32),
            pltpu.VMEM((N_HEADS, t, LANES), F32),
            pltpu.VMEM((N_HEADS, t, LANES), BF16),
        ],
        compiler_params=_cparams("parallel", "arbitrary"),
        name="sb_attention",
    )(qkv, qkv, qkv)


def _dsa_kernel(q_ref, k_ref, vt_ref, qi_ref, ki_ref, wi_ref, o_ref,
                key_ref, half_ref, bias2_ref, lga_ref, lgb_ref, p2_ref, qx_ref, qih_ref, acc_ref, m_ref, alpha_ref, tie_ref,
                *, tq, tc, ta, top_k):
    i = pl.program_id(1)
    nchunks = (i + 1) * (tq // tc)
    q_pos = i * tq + lax.broadcasted_iota(I32, (1, tq), 1)
    row = lax.broadcasted_iota(I32, (tc, 1), 0)

    qi = qi_ref[0]
    qf = q_ref[0].astype(F32)
    upper = lax.broadcasted_iota(I32, (1, LANES), 1) >= HEAD_DIM
    for h in range(N_HEADS):
        qih_ref[h] = qi[:, h * IDX_DIM:(h + 1) * IDX_DIM]
        pair = qf[:, (h // 2) * LANES:(h // 2 + 1) * LANES]
        keep = upper if h % 2 else jnp.logical_not(upper)
        qx_ref[h] = jnp.where(keep, pair, 0.0).astype(BF16)
    wt = jnp.transpose(wi_ref[0])[IDX_DIM:IDX_DIM + N_HEADS, :] * (N_HEADS ** -0.5)

    def score_chunk(c, carry):
        ks = pl.multiple_of(c * tc, tc)
        kc = ki_ref[0, pl.ds(ks, tc), :]
        sc = jnp.zeros((tc, tq), F32)
        for h in range(N_HEADS):
            lg = lax.dot_general(kc, qih_ref[h], NT_DIMS, preferred_element_type=F32)
            sc = sc + wt[h:h + 1, :] * jnp.maximum(lg, 0.0)
        sc = jnp.where(sc == 0.0, 0.0, sc)
        bits = pltpu.bitcast(sc, I32)
        skey = bits ^ ((bits >> 31) & 0x7FFFFFFF)
        key = jnp.where(ks + row <= q_pos, skey, INT_MIN)
        key_ref[pl.ds(ks, tc), :] = key
        half_ref[pl.ds(ks, tc), :] = (key >> 16).astype(I16)
        return carry

    def score_pair(c, carry):
        return score_chunk(2 * c + 1, score_chunk(2 * c, carry))

    lax.fori_loop(0, nchunks // 2, score_pair, 0)

    slab = 32

    def count(pred):
        def count_chunk(c, cnt):
            ks = pl.multiple_of(c * tc, tc)
            for u in range(tc // slab):
                cnt = cnt + jnp.where(pred(key_ref[pl.ds(ks + u * slab, slab), :]), 1, 0)
            return cnt

        cnt = lax.fori_loop(0, nchunks, count_chunk, jnp.zeros((slab, tq), I32))
        return jnp.sum(cnt.astype(F32), axis=0, keepdims=True)

    def count_half(cand):
        cand_b = jnp.broadcast_to(cand.astype(I16), (slab, tq))

        def count_chunk(c, cnt):
            ks = pl.multiple_of(c * (2 * tc), 2 * tc)
            for u in range(2 * tc // slab):
                hit = half_ref[pl.ds(ks + u * slab, slab), :] >= cand_b
                cnt = cnt + jnp.where(hit, jnp.ones((), I16), jnp.zeros((), I16))
            return cnt

        cnt = lax.fori_loop(0, nchunks // 2, count_chunk, jnp.zeros((slab, tq), I16))
        return jnp.sum(cnt.astype(F32), axis=0, keepdims=True)

    def half_search(n_above, n_ge0):
        def bit_step(b, state):
            t_u, n_ge = state
            cand = t_u | lax.shift_left(jnp.asarray(1, I32), 15 - b)
            cnt = n_above + count_half(cand - HALF_BIAS)
            ok = cnt >= top_k
            return jnp.where(ok, cand, t_u), jnp.where(ok, cnt, n_ge)

        return lax.fori_loop(0, 16, bit_step, (jnp.zeros((1, tq), I32), n_ge0))

    zero = jnp.zeros((1, tq), F32)
    hi_u, n_ge = half_search(zero, zero)
    hi_s = hi_u - HALF_BIAS
    n_above = jnp.where(hi_s == HALF_BIAS - 1, zero, count_half(jnp.minimum(hi_s + 1, HALF_BIAS - 1)))

    def lower_chunk(c, carry):
        ks = pl.multiple_of(c * tc, tc)
        kk = key_ref[pl.ds(ks, tc), :]
        low = jnp.where((kk >> 16) == hi_s, (kk & 0xFFFF) - HALF_BIAS, -HALF_BIAS)
        half_ref[pl.ds(ks, tc), :] = low.astype(I16)
        return carry

    lax.fori_loop(0, nchunks, lower_chunk, 0)
    lo_u, n_ge = half_search(n_above, n_ge)
    thr = jnp.maximum(hi_s * (2 * HALF_BIAS) + lo_u, INT_MIN + 1)
    has_ties = jnp.max(n_ge) > top_k

    def tie_quota():
        thr_b = jnp.broadcast_to(thr, (slab, tq))
        return top_k - count(lambda kk: kk > thr_b)

    quota = lax.cond(has_ties, tie_quota, lambda: jnp.zeros((1, tq), F32))

    m_ref[...] = jnp.full_like(m_ref, -jnp.inf)
    acc_ref[...] = jnp.zeros_like(acc_ref)
    tie_ref[...] = jnp.zeros_like(tie_ref)

    def select_chunk(c, slot):
        ks = pl.multiple_of(c * ta, ta)
        kk = key_ref[pl.ds(ks, ta), :]

        def plain_bias():
            return jnp.where(kk >= thr, 0.0, -jnp.inf)

        def tie_bias():
            tie = kk == thr
            before = (lax.broadcasted_iota(I32, (ta, ta), 1) <= lax.broadcasted_iota(I32, (ta, ta), 0))
            seen = tie_ref[...] + jnp.dot(before.astype(BF16), jnp.where(tie, 1.0, 0.0).astype(BF16),
                                          preferred_element_type=F32)
            tie_ref[...] = seen[ta - 1:ta, :]
            take = jnp.logical_or(kk > thr, jnp.logical_and(tie, seen <= quota))
            return jnp.where(take, 0.0, -jnp.inf)

        bias2_ref[slot] = lax.cond(has_ties, tie_bias, plain_bias)

    def attend_chunk(c, slot):
        bias_ref, lg_ref, p_ref = bias2_ref.at[slot], (lga_ref, lgb_ref)[slot], p2_ref.at[slot]
        ks = pl.multiple_of(c * ta, ta)
        for h in range(N_HEADS):
            kp = k_ref[0, pl.ds(ks, ta), (h // 2) * LANES:(h // 2 + 1) * LANES]
            lg_ref[h] = lax.dot_general(kp, qx_ref[h], NT_DIMS, preferred_element_type=F32)
        part_max = [jnp.full((HALO, tq), -jnp.inf, F32) for _ in range(N_HEADS)]
        for r in range(0, ta, HALO):
            bias = bias_ref[r:r + HALO, :]
            for h in range(N_HEADS):
                lg = lg_ref[h, r:r + HALO, :] + bias
                lg_ref[h, r:r + HALO, :] = lg
                part_max[h] = jnp.maximum(part_max[h], lg)
        for h in range(N_HEADS):
            m_old = m_ref[h:h + 1, :]
            m_new = jnp.maximum(m_old, jnp.max(part_max[h], axis=0, keepdims=True))
            m_fin = jnp.where(m_new == -jnp.inf, 0.0, m_new)
            m_rows = jnp.broadcast_to(m_fin, (BF16_ROWS, tq))
            for r in range(0, ta, BF16_ROWS):
                p_ref[h, r:r + BF16_ROWS, :] = jnp.exp2(lg_ref[h, r:r + BF16_ROWS, :] - m_rows).astype(BF16)
            alpha_ref[h:h + 1, :] = jnp.exp2(m_old - m_fin)
            m_ref[h:h + 1, :] = m_new
        for h in range(N_HEADS):
            lo, hi = h * VT_ROWS, (h + 1) * VT_ROWS
            pv = jnp.dot(vt_ref[lo:hi, pl.ds(ks, ta)], p_ref[h], preferred_element_type=F32)
            acc_ref[lo:hi, :] = alpha_ref[h:h + 1, :] * acc_ref[lo:hi, :] + pv

    def attend_pair(c, carry):
        select_chunk(2 * c, 0)
        select_chunk(2 * c + 1, 1)
        attend_chunk(2 * c, 0)
        attend_chunk(2 * c + 1, 1)
        return carry

    n_att = (i + 1) * (tq // ta)
    lax.fori_loop(0, n_att // 2, attend_pair, 0)

    @pl.when(n_att % 2 == 1)
    def _():
        select_chunk(n_att - 1, 0)
        attend_chunk(n_att - 1, 0)

    outs = []
    for h in range(N_HEADS):
        lo = h * VT_ROWS
        outs.append(acc_ref[lo:lo + HEAD_DIM, :] / acc_ref[lo + HEAD_DIM:lo + HEAD_DIM + 1, :])
    o_ref[0] = jnp.transpose(jnp.concatenate(outs, axis=0)).astype(o_ref.dtype)


def _dsa_attention(qkv, vt, k_idx, idx_kw, tq, tc, ta):
    b, s, _ = qkv.shape
    top_k = min(TOPK_MAX, s // 4)
    return pl.pallas_call(
        functools.partial(_dsa_kernel, tq=tq, tc=tc, ta=ta, top_k=top_k),
        grid=(b, s // tq),
        in_specs=[
            pl.BlockSpec((1, tq, WIDTH), lambda bi, i: (bi, i, 3)),
            pl.BlockSpec((1, s, WIDTH), lambda bi, i: (bi, 0, 4), pipeline_mode=pl.Buffered(1)),
            pl.BlockSpec((N_HEADS * VT_ROWS, s), lambda bi, i: (0, bi), pipeline_mode=pl.Buffered(1)),
            pl.BlockSpec((1, tq, WIDTH), lambda bi, i: (bi, i, 5)),
            pl.BlockSpec((1, s, IDX_DIM), lambda bi, i: (bi, 0, 0), pipeline_mode=pl.Buffered(1)),
            pl.BlockSpec((1, tq, LANES), lambda bi, i: (bi, i, 0)),
        ],
        out_specs=pl.BlockSpec((1, tq, WIDTH), lambda bi, i: (bi, i, 0)),
        out_shape=jax.ShapeDtypeStruct((b, s, WIDTH), BF16),
        scratch_shapes=[
            pltpu.VMEM((s, tq), I32),
            pltpu.VMEM((s, tq), I16),
            pltpu.VMEM((2, ta, tq), F32),
            pltpu.VMEM((N_HEADS, ta, tq), F32),
            pltpu.VMEM((N_HEADS, ta, tq), F32),
            pltpu.VMEM((2, N_HEADS, ta, tq), BF16),
            pltpu.VMEM((N_HEADS, tq, LANES), BF16),
            pltpu.VMEM((N_HEADS, tq, IDX_DIM), BF16),
            pltpu.VMEM((N_HEADS * VT_ROWS, tq), F32),
            pltpu.VMEM((N_HEADS, tq), F32),
            pltpu.VMEM((N_HEADS, tq), F32),
            pltpu.VMEM((1, tq), F32),
        ],
        compiler_params=_cparams("parallel", "arbitrary"),
        name="dsa_attention",
    )(qkv, qkv, vt, qkv, k_idx, idx_kw)


def _merge_kernel(x_ref, ya_ref, yc_ref, b_ref, c_ref, xi_ref, ch_ref, xh_ref,
                  g0_ref, g1_ref, g2_ref, cw_ref, wa_ref, wb_ref, wc_ref, wo_ref,
                  o_ref, u_ref, *, tm, seq):
    i = pl.program_id(0)
    halo_ok = ((i * tm) % seq != 0).astype(F32)
    u_ref[0:HALO, :] = ch_ref[...] * xh_ref[...] * halo_ok
    u_ref[HALO:, :] = c_ref[...] * xi_ref[...]
    cw = cw_ref[...]
    conv = (cw[0:1] * u_ref[HALO - 2:HALO - 2 + tm, :] + cw[1:2] * u_ref[HALO - 1:HALO - 1 + tm, :]
            + cw[2:3] * u_ref[HALO:, :])
    yb = (b_ref[...] * conv).astype(BF16)
    pa = jnp.dot(ya_ref[...], wa_ref[...], preferred_element_type=F32)
    pb = jnp.dot(yb, wb_ref[...], preferred_element_type=F32)
    pc = jnp.dot(yc_ref[...], wc_ref[...], preferred_element_type=F32)
    merged = (jax.nn.sigmoid(g0_ref[...]) * pa + jax.nn.sigmoid(g1_ref[...]) * pb
              + jax.nn.sigmoid(g2_ref[...]) * pc)
    o_ref[...] = x_ref[...] + jnp.dot(merged.astype(BF16), wo_ref[...], preferred_element_type=F32)


def _merge(x, ya, yc, zf, conv_w, wa, wb, wc, wo, tm, seq):
    m = x.shape[0]
    hb = tm // HALO
    row = lambda c: (lambda i: (i, c))
    halo = lambda c: (lambda i: (jnp.maximum(i * hb - 1, 0), c))
    const = lambda i: (0, 0)
    return pl.pallas_call(
        functools.partial(_merge_kernel, tm=tm, seq=seq),
        grid=(m // tm,),
        in_specs=[
            pl.BlockSpec((tm, D_MODEL), row(0)),
            pl.BlockSpec((tm, WIDTH), row(0)),
            pl.BlockSpec((tm, WIDTH), row(0)),
            pl.BlockSpec((tm, WIDTH), row(6)),
            pl.BlockSpec((tm, WIDTH), row(7)),
            pl.BlockSpec((tm, WIDTH), row(8)),
            pl.BlockSpec((HALO, WIDTH), halo(7)),
            pl.BlockSpec((HALO, WIDTH), halo(8)),
            pl.BlockSpec((tm, D_MODEL), row(0)),
            pl.BlockSpec((tm, D_MODEL), row(1)),
            pl.BlockSpec((tm, D_MODEL), row(2)),
            pl.BlockSpec((3, WIDTH), const),
            pl.BlockSpec((WIDTH, D_MODEL), const),
            pl.BlockSpec((WIDTH, D_MODEL), const),
            pl.BlockSpec((WIDTH, D_MODEL), const),
            pl.BlockSpec((D_MODEL, D_MODEL), const),
        ],
        out_specs=pl.BlockSpec((tm, D_MODEL), row(0)),
        out_shape=jax.ShapeDtypeStruct((m, D_MODEL), F32),
        scratch_shapes=[pltpu.VMEM((tm + HALO, WIDTH), F32)],
        compiler_params=_cparams("parallel"),
        name="merge",
    )(x, ya, yc, zf, zf, zf, zf, zf, zf, zf, zf, conv_w, wa, wb, wc, wo)


def _xa_kernel(x_ref, g_ref, wq_ref, k_ref, v_ref, wo_ref, o_ref):
    x = x_ref[...]
    h = _rms(x, g_ref[...]).astype(BF16)
    q = jnp.dot(h, wq_ref[...], preferred_element_type=F32) * (XA_HEAD_DIM ** -0.5)
    q = q.astype(BF16)
    outs = []
    for hh in range(XA_HEADS):
        lo, hi = hh * XA_HEAD_DIM, (hh + 1) * XA_HEAD_DIM
        lg = lax.dot_general(q[:, lo:hi], k_ref[0, :, lo:hi], NT_DIMS, preferred_element_type=F32)
        p = jnp.exp(lg - jnp.max(lg, axis=1, keepdims=True))
        p = p / jnp.sum(p, axis=1, keepdims=True)
        outs.append(jnp.dot(p.astype(BF16), v_ref[0, :, lo:hi], preferred_element_type=F32))
    o = jnp.concatenate(outs, axis=1).astype(BF16)
    o_ref[...] = x + jnp.dot(o, wo_ref[...], preferred_element_type=F32)


def _cross_attention(x, g, wq, kv, wo, tm, seq):
    m = x.shape[0]
    per_seq = seq // tm
    return pl.pallas_call(
        _xa_kernel,
        grid=(m // tm,),
        in_specs=[
            pl.BlockSpec((tm, D_MODEL), lambda i: (i, 0)),
            pl.BlockSpec((1, D_MODEL), lambda i: (0, 0)),
            pl.BlockSpec((D_MODEL, D_MODEL), lambda i: (0, 0)),
            pl.BlockSpec((1, MEM_LEN, D_MODEL), lambda i: (i // per_seq, 0, 0)),
            pl.BlockSpec((1, MEM_LEN, D_MODEL), lambda i: (i // per_seq, 0, 1)),
            pl.BlockSpec((D_MODEL, D_MODEL), lambda i: (0, 0)),
        ],
        out_specs=pl.BlockSpec((tm, D_MODEL), lambda i: (i, 0)),
        out_shape=jax.ShapeDtypeStruct((m, D_MODEL), F32),
        compiler_params=_cparams("parallel"),
        name="cross_attention",
    )(x, g.reshape(1, D_MODEL), wq, kv, kv, wo)


def _ffn_kernel(x_ref, xh_ref, g_ref, w_ref, cw_ref, cb_ref, wd_ref, gout_ref, o_ref,
                h_ref, hh_ref, a_ref, act_ref, *, tm, tf, seq, norm_out):
    i = pl.program_id(0)
    h_ref[...] = _rms(x_ref[...], g_ref[...]).astype(BF16)
    hh_ref[...] = _rms(xh_ref[...], g_ref[...]).astype(BF16)
    halo_ok = ((i * tm) % seq != 0).astype(F32)
    for n, c in enumerate(range(0, D_FF, tf)):
        a_buf = a_ref.at[n % 2]
        wa = w_ref[:, c:c + tf]
        a_buf[0:HALO, :] = jnp.dot(hh_ref[...], wa, preferred_element_type=F32) * halo_ok
        a_buf[HALO:, :] = jnp.dot(h_ref[...], wa, preferred_element_type=F32)
        u = jnp.dot(h_ref[...], w_ref[:, D_FF + c:D_FF + c + tf], preferred_element_type=F32)
        cw = cw_ref[:, c:c + tf]
        a = (cw[0:1] * a_buf[HALO - 2:HALO - 2 + tm, :] + cw[1:2] * a_buf[HALO - 1:HALO - 1 + tm, :]
             + cw[2:3] * a_buf[HALO:, :] + cb_ref[:, c:c + tf])
        act_ref[:, c:c + tf] = (jax.nn.gelu(a, approximate=True) * u).astype(BF16)
    out = x_ref[...] + jnp.dot(act_ref[...], wd_ref[...], preferred_element_type=F32)
    o_ref[...] = _rms(out, gout_ref[...]) if norm_out else out


def _conv_ffn(x, g, w_in, conv_w, conv_b, w_down, g_out, norm_out, tm, tf, seq):
    m = x.shape[0]
    hb = tm // HALO
    return pl.pallas_call(
        functools.partial(_ffn_kernel, tm=tm, tf=tf, seq=seq, norm_out=norm_out),
        grid=(m // tm,),
        in_specs=[
            pl.BlockSpec((tm, D_MODEL), lambda i: (i, 0)),
            pl.BlockSpec((HALO, D_MODEL), lambda i: (jnp.maximum(i * hb - 1, 0), 0)),
            _resident((1, D_MODEL)), _resident((D_MODEL, 2 * D_FF)), _resident((3, D_FF)),
            _resident((1, D_FF)), _resident((D_FF, D_MODEL)), _resident((1, D_MODEL)),
        ],
        out_specs=pl.BlockSpec((tm, D_MODEL), lambda i: (i, 0)),
        out_shape=jax.ShapeDtypeStruct((m, D_MODEL), F32),
        scratch_shapes=[
            pltpu.VMEM((tm, D_MODEL), BF16),
            pltpu.VMEM((HALO, D_MODEL), BF16),
            pltpu.VMEM((2, tm + HALO, tf), F32),
            pltpu.VMEM((tm, D_FF), BF16),
        ],
        compiler_params=_cparams("parallel"),
        name="conv_ffn",
    )(x, x, g.reshape(1, D_MODEL), w_in, conv_w, conv_b.reshape(1, D_FF), w_down, g_out.reshape(1, D_MODEL))


def kernel(x, mem, norm_mix, w_in, sc_conv_w, w_sb_out, w_sc_out, w_dsa_out, w_mix_o, norm_xa, norm_mem, w_xa_q, w_xa_kv, w_xa_o, norm_ffn, w_ffn_in, ffn_conv_w, ffn_conv_b, w_ffn_down, norm_final):
    batch, seq, _ = x.shape
    depth = w_in.shape[0]
    m = batch * seq
    xf = x.reshape(m, D_MODEL)
    memf = mem.reshape(batch * MEM_LEN, D_MODEL)
    o_sc, o_dsa, o_iq, o_ik, o_g = 3 * WIDTH, 6 * WIDTH, 9 * WIDTH, 10 * WIDTH, 10 * WIDTH + IDX_DIM + N_HEADS

    for l in range(depth):
        wl = w_in[l]
        qs = HEAD_DIM ** -0.5
        w_qkv = jnp.concatenate([
            wl[:, :WIDTH] * (qs * LOG2E), wl[:, WIDTH:o_sc],
            wl[:, o_dsa:o_dsa + WIDTH] * (qs * LOG2E), wl[:, o_dsa + WIDTH:o_dsa + 2 * WIDTH],
            wl[:, o_iq:o_ik] * IDX_DIM ** -0.5], axis=1).astype(BF16)
        w_v = wl[:, o_dsa + 2 * WIDTH:o_iq].T.reshape(N_HEADS, HEAD_DIM, D_MODEL)
        w_vt = jnp.pad(w_v, ((0, 0), (0, BF16_ROWS), (0, 0))).reshape(N_HEADS * VT_ROWS, D_MODEL).astype(BF16)
        ones_rows = jnp.tile(jnp.arange(VT_ROWS) >= HEAD_DIM, N_HEADS).astype(F32).reshape(N_HEADS * VT_ROWS, 1)
        w_elt = jnp.concatenate([wl[:, o_g:], wl[:, o_sc:o_dsa]], axis=1).astype(BF16)
        w_ikw = jnp.pad(wl[:, o_ik:o_g], ((0, 0), (0, LANES - (o_g - o_ik)))).astype(BF16)

        qkv, zf, vt, ikw = _mixer_proj(xf, norm_mix[l], w_qkv, w_elt, w_vt, ones_rows, w_ikw, 256, 512)
        qkv = qkv.reshape(batch, seq, 6 * WIDTH)
        ikw = ikw.reshape(batch, seq, LANES)
        k_idx = ikw[:, :, :IDX_DIM].astype(BF16)

        ya = _sb_attention(qkv, 256).reshape(m, WIDTH)
        yc = _dsa_attention(qkv, vt, k_idx, ikw, 256, 128, 256).reshape(m, WIDTH)
        xf = _merge(xf, ya, yc, zf, sc_conv_w[l], w_sb_out[l].astype(BF16), w_sc_out[l].astype(BF16),
                    w_dsa_out[l].astype(BF16), w_mix_o[l].astype(BF16), 512, seq)

        kv = _norm_matmul(memf, norm_mem[l], w_xa_kv[l].astype(BF16), BF16, 512, 512)
        xf = _cross_attention(xf, norm_xa[l], w_xa_q[l].astype(BF16), kv.reshape(batch, MEM_LEN, 2 * D_MODEL),
                              w_xa_o[l].astype(BF16), 512, seq)

        xf = _conv_ffn(xf, norm_ffn[l], w_ffn_in[l].astype(BF16), ffn_conv_w[l], ffn_conv_b[l],
                       w_ffn_down[l].astype(BF16), norm_final, l == depth - 1, 512, 256, seq)

    if depth == 0:
        xf = _rmsnorm(xf, norm_final, 512)
    return xf.reshape(batch, seq, D_MODEL)
```
